```python
import math
import jax, jax.numpy as jnp
from jax import lax
import numpy as np

D_MODEL = 2048
BATCH = 4
SEQ = 8192
DEPTH = 1

MEM_LEN = 256
SWA_HEADS = 16
SWA_KV_HEADS = 2
SWA_HEAD_DIM = 64
WINDOW = 128
BLOCK = 128
REL_BUCKETS = 32
REL_MAX_DIST = 128
GDN_HEADS = 8
GDN_HEAD_DIM = 128
GDN_CONV = 4
GDN_CHUNK = 64
MEM_HEADS = 4
MEM_HEAD_DIM = 128
D_FF = 5504
FFN_CONV = 3

NORM_EPS = 1e-5
DEEPNORM_ALPHA = (2 * DEPTH) ** 0.25
DEEPNORM_BETA = (8 * DEPTH) ** -0.25
NEG_INF = -1e30

SWA_Q = SWA_HEADS * SWA_HEAD_DIM
SWA_KV = SWA_KV_HEADS * SWA_HEAD_DIM
GDN_W = GDN_HEADS * GDN_HEAD_DIM
MEM_W = MEM_HEADS * MEM_HEAD_DIM
IN_WIDTHS = (SWA_Q, SWA_KV, SWA_KV, GDN_W, GDN_W, GDN_W, GDN_W, GDN_HEADS, GDN_HEADS, D_MODEL, D_MODEL)
IN_DIM = sum(IN_WIDTHS)

kernel_name = "hybrid_swa_gdn_gated_merge_deepnorm"


def split_columns(t, widths):
    points, acc = [], 0
    for w in widths[:-1]:
        acc += w
        points.append(acc)
    return jnp.split(t, points, axis=-1)


def layer_norm(x, g, b):
    xf = x.astype(jnp.float32)
    mu = jnp.mean(xf, axis=-1, keepdims=True)
    var = jnp.mean(jnp.square(xf - mu), axis=-1, keepdims=True)
    y = (xf - mu) * lax.rsqrt(var + NORM_EPS) * g.astype(jnp.float32) + b.astype(jnp.float32)
    return y.astype(x.dtype)


def causal_dwconv(x, w):
    width, ch = w.shape
    return lax.conv_general_dilated(
        x, w[:, None, :].astype(x.dtype), window_strides=(1,), padding=[(width - 1, 0)],
        dimension_numbers=('NWC', 'WIO', 'NWC'), feature_group_count=ch)


def t5_causal_bucket(dist):
    max_exact = REL_BUCKETS // 2
    d = jnp.maximum(dist, 1).astype(jnp.float32)
    large = max_exact + (jnp.log(d / max_exact) / math.log(REL_MAX_DIST / max_exact)
                         * (REL_BUCKETS - max_exact)).astype(jnp.int32)
    large = jnp.minimum(large, REL_BUCKETS - 1)
    return jnp.where(dist < max_exact, dist, large)


def swa_attention(q, k, v, sinks, rel_bias):
    b, s = q.shape[:2]
    nb = s // BLOCK
    grp = SWA_HEADS // SWA_KV_HEADS
    qb = q.reshape(b, nb, BLOCK, SWA_KV_HEADS, grp, SWA_HEAD_DIM)

    def band(t):
        tb = t.reshape(b, nb, BLOCK, SWA_KV_HEADS, SWA_HEAD_DIM)
        prev = jnp.pad(tb[:, :-1], ((0, 0), (1, 0), (0, 0), (0, 0), (0, 0)))
        return jnp.concatenate([prev, tb], axis=2)

    kb, vb = band(k), band(v)
    scores = jnp.einsum('bnqhgd,bnkhd->bnhgqk', qb, kb).astype(jnp.float32) * (SWA_HEAD_DIM ** -0.5)

    qi = jnp.arange(BLOCK)[:, None]
    kj = jnp.arange(2 * BLOCK)[None, :]
    dist = qi + BLOCK - kj
    in_window = (dist >= 0) & (dist < WINDOW)
    has_prev = (jnp.arange(nb)[:, None, None] > 0) | (kj >= BLOCK)[None]
    mask = in_window[None] & has_prev
    bias = rel_bias.astype(jnp.float32)[t5_causal_bucket(jnp.maximum(dist, 0))]
    bias = bias.transpose(2, 0, 1).reshape(SWA_KV_HEADS, grp, BLOCK, 2 * BLOCK)
    scores = jnp.where(mask[None, :, None, None], scores + bias, NEG_INF)

    sink = jnp.broadcast_to(sinks.astype(jnp.float32).reshape(1, 1, SWA_KV_HEADS, grp, 1, 1),
                            scores.shape[:-1] + (1,))
    probs = jax.nn.softmax(jnp.concatenate([scores, sink], axis=-1), axis=-1)[..., :-1]
    out = jnp.einsum('bnhgqk,bnkhd->bnqhgd', probs.astype(vb.dtype), vb)
    return out.reshape(b, s, SWA_Q)


def l2norm(t):
    return t * lax.rsqrt(jnp.sum(jnp.square(t), axis=-1, keepdims=True) + 1e-6)


def gated_delta_rule(q, k, v, g, beta):
    b, s, h, dk = q.shape
    dv = v.shape[-1]
    c = GDN_CHUNK
    n = s // c
    q = q * (dk ** -0.5)

    def chunk(t):
        return jnp.swapaxes(t.reshape((b, n, c, h) + t.shape[3:]), 2, 3)

    qc, kc, vc = chunk(q), chunk(k), chunk(v)
    gc = jnp.cumsum(chunk(g), axis=-1)
    bc = chunk(beta)
    kbeta = kc * bc[..., None]
    vbeta = vc * bc[..., None]

    tril = jnp.tril(jnp.ones((c, c), dtype=bool))
    strict = jnp.tril(jnp.ones((c, c), dtype=bool), -1)
    diff = gc[..., :, None] - gc[..., None, :]
    decay = jnp.where(tril, jnp.exp(jnp.where(tril, diff, 0.0)), 0.0)

    a_low = jnp.where(strict, jnp.einsum('bnhid,bnhjd->bnhij', kbeta, kc) * decay, 0.0)
    t_mat = a_low + jnp.eye(c, dtype=jnp.float32)
    rhs = jnp.concatenate([vbeta, kbeta * jnp.exp(gc)[..., None]], axis=-1)
    sol = lax.linalg.triangular_solve(t_mat, rhs, left_side=True, lower=True)
    u, w = sol[..., :dv], sol[..., dv:]

    attn_intra = jnp.where(tril, jnp.einsum('bnhid,bnhjd->bnhij', qc, kc) * decay, 0.0)
    q_dec = qc * jnp.exp(gc)[..., None]
    k_dec = kc * jnp.exp(gc[..., -1:] - gc)[..., None]
    g_last = jnp.exp(gc[..., -1])

    def step(state, inp):
        qd, kd, uu, ww, ai, gl = inp
        v_new = uu - jnp.einsum('bhck,bhkv->bhcv', ww, state)
        o = jnp.einsum('bhck,bhkv->bhcv', qd, state) + jnp.einsum('bhij,bhjv->bhiv', ai, v_new)
        state = state * gl[..., None, None] + jnp.einsum('bhck,bhcv->bhkv', kd, v_new)
        return state, o

    xs = tuple(jnp.moveaxis(t, 1, 0) for t in (q_dec, k_dec, u, w, attn_intra, g_last))
    state0 = jnp.zeros((b, h, dk, dv), jnp.float32)
    _, o = lax.scan(step, state0, xs)
    return jnp.swapaxes(jnp.moveaxis(o, 0, 1), 2, 3).reshape(b, s, h, dv)


def gated_deltanet(gq, gk, gv, gz, gb, ga, conv_w, a_log, dt_bias, norm_w):
    b, s, _ = gq.shape
    dtype = gq.dtype
    qkv = jax.nn.silu(causal_dwconv(jnp.concatenate([gq, gk, gv], axis=-1), conv_w))
    q, k, v = split_columns(qkv.astype(jnp.float32), (GDN_W, GDN_W, GDN_W))
    q = l2norm(q.reshape(b, s, GDN_HEADS, GDN_HEAD_DIM))
    k = l2norm(k.reshape(b, s, GDN_HEADS, GDN_HEAD_DIM))
    v = v.reshape(b, s, GDN_HEADS, GDN_HEAD_DIM)
    beta = jax.nn.sigmoid(gb.astype(jnp.float32))
    g = -jnp.exp(a_log.astype(jnp.float32)) * jax.nn.softplus(ga.astype(jnp.float32) + dt_bias.astype(jnp.float32))
    o = gated_delta_rule(q, k, v, g, beta)
    o = o * lax.rsqrt(jnp.mean(jnp.square(o), axis=-1, keepdims=True) + 1e-6) * norm_w.astype(jnp.float32)
    z = jax.nn.silu(gz.astype(jnp.float32)).reshape(b, s, GDN_HEADS, GDN_HEAD_DIM)
    return (o * z).reshape(b, s, GDN_W).astype(dtype)


def memory_attention(x, mem, w_q, w_kv, w_o):
    b, s, _ = x.shape
    m = mem.shape[1]
    q = (x @ w_q).reshape(b, s, MEM_HEADS, MEM_HEAD_DIM)
    k, v = split_columns(mem @ w_kv, (MEM_W, MEM_W))
    k = k.reshape(b, m, MEM_HEADS, MEM_HEAD_DIM)
    v = v.reshape(b, m, MEM_HEADS, MEM_HEAD_DIM)
    scores = jnp.einsum('bshd,bmhd->bhsm', q, k).astype(jnp.float32) * (MEM_HEAD_DIM ** -0.5)
    p = jax.nn.softmax(scores, axis=-1)
    o = jnp.einsum('bhsm,bmhd->bshd', p.astype(v.dtype), v).reshape(b, s, MEM_W)
    return o @ w_o


def setup_inputs(seed: int = 0) -> dict:
    key = jax.random.key(seed)
    ks = jax.random.split(key, 32)
    f32 = jnp.float32
    L = DEPTH

    def nrm(k, shape, scale):
        return jax.random.normal(k, shape, f32) * scale

    dt = jnp.exp(jax.random.uniform(ks[7], (L, GDN_HEADS), f32, math.log(1e-3), math.log(1e-1)))
    return {
        "x": nrm(ks[0], (BATCH, SEQ, D_MODEL), 1.0),
        "mem": nrm(ks[1], (BATCH, MEM_LEN, D_MODEL), 1.0),
        "w_in": nrm(ks[2], (L, D_MODEL, IN_DIM), D_MODEL ** -0.5),
        "rel_bias": nrm(ks[3], (REL_BUCKETS, SWA_HEADS), 0.5),
        "swa_sinks": nrm(ks[4], (L, SWA_HEADS), 1.0),
        "gdn_conv_w": nrm(ks[5], (L, GDN_CONV, 3 * GDN_W), GDN_CONV ** -0.5),
        "gdn_a_log": jnp.log(jax.random.uniform(ks[6], (L, GDN_HEADS), f32, 1.0, 16.0)),
        "gdn_dt_bias": dt + jnp.log(-jnp.expm1(-dt)),
        "gdn_norm_w": 1.0 + nrm(ks[8], (L, GDN_HEAD_DIM), 0.02),
        "w_br_swa": nrm(ks[9], (L, SWA_Q, D_MODEL), SWA_Q ** -0.5),
        "w_br_gdn": nrm(ks[10], (L, GDN_W, D_MODEL), GDN_W ** -0.5),
        "w_mix_o": nrm(ks[11], (L, D_MODEL, D_MODEL), D_MODEL ** -0.5 * DEEPNORM_BETA),
        "ln1_g": 1.0 + nrm(ks[12], (L, D_MODEL), 0.02),
        "ln1_b": nrm(ks[13], (L, D_MODEL), 0.02),
        "w_mem_q": nrm(ks[14], (L, D_MODEL, MEM_W), D_MODEL ** -0.5),
        "w_mem_kv": nrm(ks[15], (L, D_MODEL, 2 * MEM_W), D_MODEL ** -0.5),
        "w_mem_o": nrm(ks[16], (L, MEM_W, D_MODEL), MEM_W ** -0.5 * DEEPNORM_BETA),
        "ln2_g": 1.0 + nrm(ks[17], (L, D_MODEL), 0.02),
        "ln2_b": nrm(ks[18], (L, D_MODEL), 0.02),
        "w_up": nrm(ks[19], (L, D_MODEL, 2 * D_FF), D_MODEL ** -0.5),
        "ffn_conv_w": nrm(ks[20], (L, FFN_CONV, 2 * D_FF), FFN_CONV ** -0.5),
        "ffn_conv_b": nrm(ks[21], (L, 2 * D_FF), 0.02),
        "w_down": nrm(ks[22], (L, D_FF, D_MODEL), D_FF ** -0.5 * DEEPNORM_BETA),
        "ln3_g": 1.0 + nrm(ks[23], (L, D_MODEL), 0.02),
        "ln3_b": nrm(ks[24], (L, D_MODEL), 0.02),
    }


def reference(x, mem, w_in, rel_bias, swa_sinks, gdn_conv_w, gdn_a_log, gdn_dt_bias, gdn_norm_w,
              w_br_swa, w_br_gdn, w_mix_o, ln1_g, ln1_b, w_mem_q, w_mem_kv, w_mem_o, ln2_g, ln2_b,
              w_up, ffn_conv_w, ffn_conv_b, w_down, ln3_g, ln3_b):
    b, s, _ = x.shape
    for l in range(DEPTH):
        proj = x @ w_in[l]
        (sq, sk, sv, gq, gk, gv, gz, gb, ga, gate_swa, gate_gdn) = split_columns(proj, IN_WIDTHS)
        y_swa = swa_attention(
            sq.reshape(b, s, SWA_HEADS, SWA_HEAD_DIM),
            sk.reshape(b, s, SWA_KV_HEADS, SWA_HEAD_DIM),
            sv.reshape(b, s, SWA_KV_HEADS, SWA_HEAD_DIM),
            swa_sinks[l], rel_bias) @ w_br_swa[l]
        y_gdn = gated_deltanet(gq, gk, gv, gz, gb, ga, gdn_conv_w[l], gdn_a_log[l],
                               gdn_dt_bias[l], gdn_norm_w[l]) @ w_br_gdn[l]
        mixed = jax.nn.sigmoid(gate_swa) * y_swa + jax.nn.sigmoid(gate_gdn) * y_gdn
        x = layer_norm(DEEPNORM_ALPHA * x + mixed @ w_mix_o[l], ln1_g[l], ln1_b[l])
        c = memory_attention(x, mem, w_mem_q[l], w_mem_kv[l], w_mem_o[l])
        x = layer_norm(DEEPNORM_ALPHA * x + c, ln2_g[l], ln2_b[l])
        hcat = causal_dwconv(x @ w_up[l], ffn_conv_w[l]) + ffn_conv_b[l]
        h_gate, h_up = split_columns(hcat, (D_FF, D_FF))
        f = (jax.nn.silu(h_gate) * h_up) @ w_down[l]
        x = layer_norm(DEEPNORM_ALPHA * x + f, ln3_g[l], ln3_b[l])
    return x
```

```python
import functools
import math

import jax
import jax.numpy as jnp
from jax import lax
from jax.experimental import pallas as pl
from jax.experimental.pallas import tpu as pltpu

F32 = jnp.float32
BF16 = jnp.bfloat16

LANES = 128
BF16_ROWS = 16
VMEM_LIMIT = 56 * 1024 * 1024

SWA_HEADS = 16
SWA_KV_HEADS = 2
SWA_HEAD_DIM = 64
SWA_BLOCK = 128
REL_BUCKETS = 32
REL_MAX_DIST = 128
GDN_HEADS = 8
GDN_HEAD_DIM = 128
GDN_CONV = 4
GDN_CHUNK = 64
GDN_PAIR = 2 * GDN_CHUNK
MEM_HEADS = 4
MEM_HEAD_DIM = 128
FFN_CONV = 3
NORM_EPS = 1e-5
NEG_INF = -1e30


def _params(*sem):
    return pltpu.CompilerParams(dimension_semantics=sem, vmem_limit_bytes=VMEM_LIMIT)


def _dot(a, b):
    return jnp.dot(a.astype(BF16), b.astype(BF16), preferred_element_type=F32)


def _dot_nt(a, b):
    return lax.dot_general(a.astype(BF16), b.astype(BF16), (((1,), (1,)), ((), ())),
                           preferred_element_type=F32)


def _split(a):
    hi = a.astype(BF16)
    lo = (a - hi.astype(F32)).astype(BF16)
    return hi, lo


def _dot3(a, b):
    ah, al = _split(a)
    bh, bl = _split(b)
    d = functools.partial(jnp.dot, preferred_element_type=F32)
    return d(ah, bh) + (d(al, bh) + d(ah, bl))


def _sigmoid(x):
    return 1.0 / (1.0 + jnp.exp(-x))


def _silu(x):
    return x * _sigmoid(x)


def _layer_norm(y, g, b):
    mu = jnp.mean(y, axis=-1, keepdims=True)
    yc = y - mu
    var = jnp.mean(yc * yc, axis=-1, keepdims=True)
    return yc * lax.rsqrt(var + NORM_EPS) * g + b


def _mm_kernel(x_ref, w_ref, o_ref):
    o_ref[...] = jnp.dot(x_ref[...], w_ref[...], preferred_element_type=F32).astype(o_ref.dtype)


def _matmul(x, w, tm, tn, out_dtype):
    m, k = x.shape
    n = w.shape[1]
    assert m % tm == 0 and n % tn == 0
    return pl.pallas_call(
        _mm_kernel,
        grid=(m // tm, n // tn),
        in_specs=[pl.BlockSpec((tm, k), lambda i, j: (i, 0)),
                  pl.BlockSpec((k, tn), lambda i, j: (0, j))],
        out_specs=pl.BlockSpec((tm, tn), lambda i, j: (i, j)),
        out_shape=jax.ShapeDtypeStruct((m, n), out_dtype),
        compiler_params=_params("parallel", "arbitrary"),
        name="matmul",
    )(x, w)


def _t5_causal_bucket(dist):
    max_exact = REL_BUCKETS // 2
    d = jnp.maximum(dist, 1).astype(F32)
    large = max_exact + (jnp.log(d / max_exact) / math.log(REL_MAX_DIST / max_exact)
                         * (REL_BUCKETS - max_exact)).astype(jnp.int32)
    large = jnp.minimum(large, REL_BUCKETS - 1)
    return jnp.where(dist < max_exact, dist, large)


def _swa_bias_table(rel_bias):
    qi = jnp.arange(SWA_BLOCK)[:, None]
    kj = jnp.arange(2 * SWA_BLOCK)[None, :]
    dist = qi + SWA_BLOCK - kj
    in_window = (dist >= 0) & (dist < SWA_BLOCK)
    bias = rel_bias.astype(F32)[_t5_causal_bucket(jnp.maximum(dist, 0))]
    bias = bias.transpose(2, 0, 1)
    return jnp.where(in_window[None], bias, NEG_INF)


def _swa_kernel(sink_ref, q_ref, kv_ref, pkv_ref, bias_ref, o_ref, *, blocks_per_batch):
    half = SWA_HEAD_DIM
    first = (pl.program_id(0) % blocks_per_batch) == 0
    kv = jnp.concatenate([pkv_ref[...], kv_ref[...]], axis=0)
    k = kv[:, :LANES]
    v = kv[:, LANES:]
    lo = lax.broadcasted_iota(jnp.int32, k.shape, 1) < half
    zero = jnp.zeros_like(k)

    def swap(t):
        return jnp.concatenate([t[:, half:], t[:, :half]], axis=1)

    def padded(t):
        tr = swap(t)
        return ((jnp.where(lo, t, zero), jnp.where(lo, zero, tr)),
                (jnp.where(lo, tr, zero), jnp.where(lo, zero, t)))

    kpad = padded(k)
    vpad = padded(v)
    col = lax.broadcasted_iota(jnp.int32, (SWA_BLOCK, 2 * SWA_BLOCK), 1)
    no_prev = jnp.logical_and(first, col < SWA_BLOCK)
    out_lo = lax.broadcasted_iota(jnp.int32, (SWA_BLOCK, LANES), 1) < half
    pairs = SWA_HEADS // 2
    for p in range(pairs):
        kvh = (2 * p) // (SWA_HEADS // SWA_KV_HEADS)
        qc = q_ref[:, p * LANES:(p + 1) * LANES]
        acc = None
        inv = []
        for par in range(2):
            head = 2 * p + par
            s = _dot_nt(qc, kpad[kvh][par]) * (SWA_HEAD_DIM ** -0.5) + bias_ref[head]
            s = jnp.where(no_prev, NEG_INF, s)
            sink = sink_ref[head]
            m = jnp.maximum(jnp.max(s, axis=-1, keepdims=True), sink)
            e = jnp.exp(s - m)
            d = jnp.sum(e, axis=-1, keepdims=True) + jnp.exp(sink - m)
            pv = _dot(e, vpad[kvh][par])
            acc = pv if acc is None else acc + pv
            inv.append(1.0 / d)
        o_ref[:, p * LANES:(p + 1) * LANES] = (acc * jnp.where(out_lo, inv[0], inv[1])).astype(o_ref.dtype)


def _swa_attention(proj, q_col, kv_col, sinks, bias_tab, seq):
    t = proj.shape[0]
    qw = SWA_HEADS * SWA_HEAD_DIM
    kvw = 2 * SWA_KV_HEADS * SWA_HEAD_DIM
    bpb = seq // SWA_BLOCK
    return pl.pallas_call(
        functools.partial(_swa_kernel, blocks_per_batch=bpb),
        grid=(t // SWA_BLOCK,),
        in_specs=[pl.BlockSpec(memory_space=pltpu.SMEM),
                  pl.BlockSpec((SWA_BLOCK, qw), lambda i: (i, q_col)),
                  pl.BlockSpec((SWA_BLOCK, kvw), lambda i: (i, kv_col)),
                  pl.BlockSpec((SWA_BLOCK, kvw), lambda i: (jnp.maximum(i - 1, 0), kv_col)),
                  pl.BlockSpec((SWA_HEADS, SWA_BLOCK, 2 * SWA_BLOCK), lambda i: (0, 0, 0))],
        out_specs=pl.BlockSpec((SWA_BLOCK, qw), lambda i: (i, 0)),
        out_shape=jax.ShapeDtypeStruct((t, qw), BF16),
        compiler_params=_params("parallel"),
        name="swa_attention",
    )(sinks, proj, proj, proj, bias_tab)


def _gdn_pre_kernel(q_ref, k_ref, v_ref, hq_ref, hk_ref, hv_ref, ba_ref, cw_ref, alog_ref, dtb_ref,
                    u_ref, w_ref, qd_ref, ai_ref, kdt_ref, gl_ref, *, pairs_per_batch):
    n = GDN_PAIR
    c = GDN_CHUNK
    first = (pl.program_id(0) % pairs_per_batch) == 0
    row = lax.broadcasted_iota(jnp.int32, (n, n), 0)
    col = lax.broadcasted_iota(jnp.int32, (n, n), 1)
    same = (row >= c) == (col >= c)
    tril = jnp.logical_and(same, row >= col)
    strict = jnp.logical_and(same, row > col)
    eye = jnp.where(row == col, 1.0, 0.0).astype(F32)
    rowc = lax.broadcasted_iota(jnp.int32, (n, 1), 0)

    ba = ba_ref[...]
    beta_all = _sigmoid(ba)
    x = ba + dtb_ref[...]
    softplus = jnp.maximum(x, 0.0) + jnp.log1p(jnp.exp(-jnp.abs(x)))
    g_all = -jnp.exp(alog_ref[...]) * softplus
    gc_all = _dot3(jnp.where(tril, 1.0, 0.0).astype(F32), g_all)
    gct_all = gc_all.T

    def conv_silu(cur_ref, halo_ref, which, sl):
        cur = cur_ref[:, sl].astype(F32)
        halo = jnp.where(first, 0.0, halo_ref[:, sl].astype(F32))
        ext = jnp.concatenate([halo, cur], axis=0)
        base = which * GDN_HEADS * GDN_HEAD_DIM
        cw = cw_ref[:, base + sl.start:base + sl.stop]
        y = cw[GDN_CONV - 1:GDN_CONV] * cur
        for j in range(1, GDN_CONV):
            y = y + cw[GDN_CONV - 1 - j:GDN_CONV - j] * pltpu.roll(ext, j, axis=0)[BF16_ROWS:]
        return _silu(y)

    def l2norm(t):
        return t * lax.rsqrt(jnp.sum(t * t, axis=-1, keepdims=True) + 1e-6)

    for h in range(GDN_HEADS):
        sl = slice(h * LANES, (h + 1) * LANES)
        q = l2norm(conv_silu(q_ref, hq_ref, 0, sl)) * (GDN_HEAD_DIM ** -0.5)
        k = l2norm(conv_silu(k_ref, hk_ref, 1, sl))
        v = conv_silu(v_ref, hv_ref, 2, sl)
        beta = beta_all[:, h:h + 1]
        gc = gc_all[:, GDN_HEADS + h:GDN_HEADS + h + 1]
        gcr = gct_all[GDN_HEADS + h:GDN_HEADS + h + 1, :]
        kb = k * beta
        vb = v * beta
        kq = _dot_nt(jnp.concatenate([kb, q], axis=0), k)
        decay = jnp.where(tril, jnp.exp(jnp.where(tril, gc - gcr, 0.0)), 0.0)
        a = jnp.where(strict, kq[:n] * decay, 0.0)
        ai = kq[n:] * decay
        p = eye - a
        ak = _dot3(a, a)
        for _ in range(4):
            m = _dot3(jnp.concatenate([ak, p], axis=0), ak)
            ak = m[:n]
            p = p + m[n:]
        p = p + _dot3(p, ak)
        egc = jnp.exp(gc)
        sol = _dot3(p, jnp.concatenate([vb, kb * egc], axis=1))
        gl_e = gc[c - 1:c, :]
        gl_o = gc[n - 1:n, :]
        kd = k * jnp.exp(jnp.where(rowc < c, gl_e, gl_o) - gc)
        u_ref[:, sl] = sol[:, :LANES]
        w_ref[:, sl] = sol[:, LANES:].astype(w_ref.dtype)
        qd_ref[:, sl] = (q * egc).astype(qd_ref.dtype)
        ai_ref[:, sl] = ai.astype(ai_ref.dtype)
        kdt_ref[:, sl] = kd.T.astype(kdt_ref.dtype)
        gl_ref[0, 0:1, sl] = jnp.broadcast_to(jnp.exp(gl_e), (1, LANES))
        gl_ref[0, 1:2, sl] = jnp.broadcast_to(jnp.exp(gl_o), (1, LANES))


def _gdn_prepass(proj, ba, conv_w, alog_row, dtb_row, seq):
    t = proj.shape[0]
    gw = GDN_HEADS * GDN_HEAD_DIM
    n = GDN_PAIR
    ppb = seq // n
    hb = n // BF16_ROWS

    def cur(cb):
        return pl.BlockSpec((n, gw), lambda i: (i, cb))

    def halo(cb):
        return pl.BlockSpec((BF16_ROWS, gw), lambda i: (jnp.maximum(i * hb - 1, 0), cb))

    row_spec = pl.BlockSpec((n, gw), lambda i: (i, 0))
    return pl.pallas_call(
        functools.partial(_gdn_pre_kernel, pairs_per_batch=ppb),
        grid=(t // n,),
        in_specs=[cur(0), cur(1), cur(2), halo(0), halo(1), halo(2),
                  pl.BlockSpec((n, LANES), lambda i: (i, 0)),
                  pl.BlockSpec((GDN_CONV, 3 * gw), lambda i: (0, 0)),
                  pl.BlockSpec((1, LANES), lambda i: (0, 0)),
                  pl.BlockSpec((1, LANES), lambda i: (0, 0))],
        out_specs=[row_spec, row_spec, row_spec, row_spec, row_spec,
                   pl.BlockSpec((1, 2, gw), lambda i: (i, 0, 0))],
        out_shape=[jax.ShapeDtypeStruct((t, gw), F32),
                   jax.ShapeDtypeStruct((t, gw), BF16),
                   jax.ShapeDtypeStruct((t, gw), BF16),
                   jax.ShapeDtypeStruct((t, gw), BF16),
                   jax.ShapeDtypeStruct((t, gw), BF16),
                   jax.ShapeDtypeStruct((t // n, 2, gw), F32)],
        compiler_params=_params("parallel"),
        name="gdn_prepass",
    )(proj, proj, proj, proj, proj, proj, ba, conv_w, alog_row, dtb_row)


def _gdn_scan_kernel(u_ref, w_ref, qd_ref, ai_ref, kdt_ref, gl_ref, z_ref, nw_ref, o_ref, st_ref):
    c = GDN_CHUNK

    @pl.when(pl.program_id(1) == 0)
    def _():
        st_ref[...] = jnp.zeros_like(st_ref)

    zeros = jnp.zeros((c, LANES), F32)
    nw = nw_ref[...]
    for h in range(GDN_HEADS):
        sl = slice(h * LANES, (h + 1) * LANES)
        s = st_ref[h]
        kdt = kdt_ref[:, sl]
        for ci in range(2):
            rs = slice(ci * c, (ci + 1) * c)
            sb = s.astype(BF16)
            vnew = u_ref[rs, sl] - _dot(w_ref[rs, sl], sb)
            vpad = jnp.concatenate([vnew, zeros] if ci == 0 else [zeros, vnew], axis=0).astype(BF16)
            o = _dot(qd_ref[rs, sl], sb) + _dot(ai_ref[rs, sl], vpad)
            s = s * gl_ref[0, ci:ci + 1, sl] + _dot(kdt, vpad)
            z = z_ref[rs, sl].astype(F32)
            o = o * lax.rsqrt(jnp.mean(o * o, axis=-1, keepdims=True) + 1e-6) * nw * _silu(z)
            o_ref[rs, sl] = o.astype(o_ref.dtype)
        st_ref[h] = s


def _gdn_scan(u, w, qd, ai, kdt, gl, proj, z_col, norm_w_row, batch, seq):
    t = u.shape[0]
    gw = GDN_HEADS * GDN_HEAD_DIM
    n = GDN_PAIR
    ppb = seq // n
    row_spec = pl.BlockSpec((n, gw), lambda b, i: (b * ppb + i, 0))
    return pl.pallas_call(
        _gdn_scan_kernel,
        grid=(batch, ppb),
        in_specs=[row_spec, row_spec, row_spec, row_spec, row_spec,
                  pl.BlockSpec((1, 2, gw), lambda b, i: (b * ppb + i, 0, 0)),
                  pl.BlockSpec((n, gw), lambda b, i: (b * ppb + i, z_col)),
                  pl.BlockSpec((1, LANES), lambda b, i: (0, 0))],
        out_specs=row_spec,
        out_shape=jax.ShapeDtypeStruct((t, gw), BF16),
        scratch_shapes=[pltpu.VMEM((GDN_HEADS, GDN_HEAD_DIM, GDN_HEAD_DIM), F32)],
        compiler_params=_params("arbitrary", "arbitrary"),
        name="gdn_scan",
    )(u, w, qd, ai, kdt, gl, proj, norm_w_row)


def _merge_kernel(x_ref, a_ref, g_ref, wgs_ref, wgg_ref, wbs_ref, wbg_ref, o_ref):
    x = x_ref[...]
    gate_s = _sigmoid(jnp.dot(x, wgs_ref[...], preferred_element_type=F32))
    gate_g = _sigmoid(jnp.dot(x, wgg_ref[...], preferred_element_type=F32))
    y_s = jnp.dot(a_ref[...], wbs_ref[...], preferred_element_type=F32)
    y_g = jnp.dot(g_ref[...], wbg_ref[...], preferred_element_type=F32)
    o_ref[...] = (gate_s * y_s + gate_g * y_g).astype(o_ref.dtype)


def _merge(xb, attn, gdn, wgs, wgg, wbs, wbg, tm, tn):
    t, d = xb.shape
    ws = attn.shape[1]
    wg = gdn.shape[1]
    return pl.pallas_call(
        _merge_kernel,
        grid=(t // tm, d // tn),
        in_specs=[pl.BlockSpec((tm, d), lambda i, j: (i, 0)),
                  pl.BlockSpec((tm, ws), lambda i, j: (i, 0)),
                  pl.BlockSpec((tm, wg), lambda i, j: (i, 0)),
                  pl.BlockSpec((d, tn), lambda i, j: (0, j)),
                  pl.BlockSpec((d, tn), lambda i, j: (0, j)),
                  pl.BlockSpec((ws, tn), lambda i, j: (0, j)),
                  pl.BlockSpec((wg, tn), lambda i, j: (0, j))],
        out_specs=pl.BlockSpec((tm, tn), lambda i, j: (i, j)),
        out_shape=jax.ShapeDtypeStruct((t, d), BF16),
        compiler_params=_params("parallel", "arbitrary"),
        name="branch_merge",
    )(xb, attn, gdn, wgs, wgg, wbs, wbg)


def _mixo_ln_kernel(m_ref, x_ref, w_ref, g_ref, b_ref, o_ref, *, alpha):
    y = alpha * x_ref[...] + jnp.dot(m_ref[...], w_ref[...], preferred_element_type=F32)
    o_ref[...] = _layer_norm(y, g_ref[...], b_ref[...])


def _mixo_ln(mixed, x, w, g, b, alpha, tm):
    t, d = x.shape
    vec = pl.BlockSpec((1, d), lambda i: (0, 0))
    return pl.pallas_call(
        functools.partial(_mixo_ln_kernel, alpha=alpha),
        grid=(t // tm,),
        in_specs=[pl.BlockSpec((tm, d), lambda i: (i, 0)),
                  pl.BlockSpec((tm, d), lambda i: (i, 0)),
                  pl.BlockSpec((d, d), lambda i: (0, 0)),
                  vec, vec],
        out_specs=pl.BlockSpec((tm, d), lambda i: (i, 0)),
        out_shape=jax.ShapeDtypeStruct((t, d), F32),
        compiler_params=_params("parallel"),
        name="mixo_layernorm",
    )(mixed, x, w, g, b)


def _mem_kernel(x_ref, wq_ref, kv_ref, wo_ref, g_ref, b_ref, o_ref, *, alpha):
    x = x_ref[...]
    q = jnp.dot(x.astype(BF16), wq_ref[...], preferred_element_type=F32)
    mw = MEM_HEADS * MEM_HEAD_DIM
    outs = []
    for h in range(MEM_HEADS):
        sl = slice(h * MEM_HEAD_DIM, (h + 1) * MEM_HEAD_DIM)
        s = _dot_nt(q[:, sl], kv_ref[:, sl]) * (MEM_HEAD_DIM ** -0.5)
        m = jnp.max(s, axis=-1, keepdims=True)
        e = jnp.exp(s - m)
        d = jnp.sum(e, axis=-1, keepdims=True)
        vh = kv_ref[:, mw + h * MEM_HEAD_DIM:mw + (h + 1) * MEM_HEAD_DIM]
        outs.append(_dot(e, vh) / d)
    o = jnp.concatenate(outs, axis=1)
    y = alpha * x + _dot(o, wo_ref[...])
    o_ref[...] = _layer_norm(y, g_ref[...], b_ref[...])


def _mem_attention_ln(x, wq, kv, wo, g, b, alpha, batch, seq, mem_len, tm):
    t, d = x.shape
    mw = MEM_HEADS * MEM_HEAD_DIM
    tpb = seq // tm
    vec = pl.BlockSpec((1, d), lambda bi, i: (0, 0))
    return pl.pallas_call(
        functools.partial(_mem_kernel, alpha=alpha),
        grid=(batch, tpb),
        in_specs=[pl.BlockSpec((tm, d), lambda bi, i: (bi * tpb + i, 0)),
                  pl.BlockSpec((d, mw), lambda bi, i: (0, 0)),
                  pl.BlockSpec((mem_len, 2 * mw), lambda bi, i: (bi, 0)),
                  pl.BlockSpec((mw, d), lambda bi, i: (0, 0)),
                  vec, vec],
        out_specs=pl.BlockSpec((tm, d), lambda bi, i: (bi * tpb + i, 0)),
        out_shape=jax.ShapeDtypeStruct((t, d), F32),
        compiler_params=_params("parallel", "arbitrary"),
        name="mem_attention_layernorm",
    )(x, wq, kv, wo, g, b)


def _ffn_kernel(x_ref, halo_ref, wg_ref, wu_ref, cg_ref, cu_ref, bg_ref, bu_ref, wd_ref, g_ref, b_ref,
                o_ref, xb_ref, acc_ref, *, alpha, tiles_per_batch):
    j = pl.program_id(1)
    hr = BF16_ROWS

    @pl.when(j == 0)
    def _():
        first = (pl.program_id(0) % tiles_per_batch) == 0
        xb_ref[:hr, :] = jnp.where(first, 0.0, halo_ref[...]).astype(BF16)
        xb_ref[hr:, :] = x_ref[...].astype(BF16)
        acc_ref[...] = jnp.zeros_like(acc_ref)

    xb = xb_ref[...]

    def conv(w_ref, cw_ref, bias_ref):
        h = jnp.dot(xb, w_ref[...], preferred_element_type=F32)
        cw = cw_ref[...]
        y = cw[FFN_CONV - 1:FFN_CONV] * h[hr:] + bias_ref[...]
        for k in range(1, FFN_CONV):
            y = y + cw[FFN_CONV - 1 - k:FFN_CONV - k] * pltpu.roll(h, k, axis=0)[hr:]
        return y

    act = _silu(conv(wg_ref, cg_ref, bg_ref)) * conv(wu_ref, cu_ref, bu_ref)
    acc_ref[...] += _dot(act, wd_ref[...])

    @pl.when(j == pl.num_programs(1) - 1)
    def _():
        y = alpha * x_ref[...] + acc_ref[...]
        o_ref[...] = _layer_norm(y, g_ref[...], b_ref[...])


def _ffn_ln(x, wg, wu, cg, cu, bg, bu, wd, g, b, alpha, seq, tm, tn):
    t, d = x.shape
    ff = wg.shape[1]
    tpb = seq // tm
    hb = tm // BF16_ROWS
    vec = pl.BlockSpec((1, d), lambda i, j: (0, 0))
    return pl.pallas_call(
        functools.partial(_ffn_kernel, alpha=alpha, tiles_per_batch=tpb),
        grid=(t // tm, ff // tn),
        in_specs=[pl.BlockSpec((tm, d), lambda i, j: (i, 0)),
                  pl.BlockSpec((BF16_ROWS, d), lambda i, j: (jnp.maximum(i * hb - 1, 0), 0)),
                  pl.BlockSpec((d, tn), lambda i, j: (0, j)),
                  pl.BlockSpec((d, tn), lambda i, j: (0, j)),
                  pl.BlockSpec((FFN_CONV, tn), lambda i, j: (0, j)),
                  pl.BlockSpec((FFN_CONV, tn), lambda i, j: (0, j)),
                  pl.BlockSpec((1, tn), lambda i, j: (0, j)),
                  pl.BlockSpec((1, tn), lambda i, j: (0, j)),
                  pl.BlockSpec((tn, d), lambda i, j: (j, 0)),
                  vec, vec],
        out_specs=pl.BlockSpec((tm, d), lambda i, j: (i, 0)),
        out_shape=jax.ShapeDtypeStruct((t, d), F32),
        scratch_shapes=[pltpu.VMEM((BF16_ROWS + tm, d), BF16),
                        pltpu.VMEM((tm, d), F32)],
        compiler_params=_params("parallel", "arbitrary"),
        name="ffn_layernorm",
    )(x, x, wg, wu, cg, cu, bg, bu, wd, g, b)


def _pad_cols(a, n):
    return jnp.pad(a, ((0, 0), (0, n - a.shape[1])))


def _layer(x2d, mem2d, batch, seq, mem_len, p, bias_tab, alpha, tiles):
    d = x2d.shape[1]
    sq = SWA_HEADS * SWA_HEAD_DIM
    skv = SWA_KV_HEADS * SWA_HEAD_DIM
    gw = GDN_HEADS * GDN_HEAD_DIM
    w_in = p["w_in"]
    o_gdn = sq + 2 * skv
    o_ba = o_gdn + 4 * gw
    o_gate = o_ba + 2 * GDN_HEADS
    w_proj = jnp.concatenate([w_in[:, o_gdn:o_ba], w_in[:, :o_gdn]], axis=1).astype(BF16)
    w_ba = _pad_cols(w_in[:, o_ba:o_gate], LANES).astype(BF16)
    w_gs = w_in[:, o_gate:o_gate + d].astype(BF16)
    w_gg = w_in[:, o_gate + d:o_gate + 2 * d].astype(BF16)

    xb = x2d.astype(BF16)
    proj = _matmul(xb, w_proj, tiles["proj_tm"], tiles["proj_tn"], BF16)
    ba = _matmul(xb, w_ba, tiles["proj_tm"], LANES, F32)

    attn = _swa_attention(proj, (4 * gw) // sq, (4 * gw + sq) // (2 * skv), p["swa_sinks"], bias_tab, seq)

    zpad = jnp.zeros((GDN_HEADS,), F32)
    alog_row = _pad_cols(jnp.concatenate([zpad, p["gdn_a_log"]])[None, :], LANES)
    dtb_row = _pad_cols(jnp.concatenate([zpad, p["gdn_dt_bias"]])[None, :], LANES)
    u, w, qd, ai, kdt, gl = _gdn_prepass(proj, ba, p["gdn_conv_w"], alog_row, dtb_row, seq)
    gdn = _gdn_scan(u, w, qd, ai, kdt, gl, proj, 3, p["gdn_norm_w"][None, :], batch, seq)

    mixed = _merge(xb, attn, gdn, w_gs, w_gg, p["w_br_swa"].astype(BF16), p["w_br_gdn"].astype(BF16),
                   tiles["merge_tm"], tiles["merge_tn"])
    x1 = _mixo_ln(mixed, x2d, p["w_mix_o"].astype(BF16), p["ln1_g"][None, :], p["ln1_b"][None, :],
                  alpha, tiles["ln_tm"])

    kv = _matmul(mem2d.astype(BF16), p["w_mem_kv"].astype(BF16), mem_len, tiles["kv_tn"], BF16)
    x2 = _mem_attention_ln(x1, p["w_mem_q"].astype(BF16), kv, p["w_mem_o"].astype(BF16),
                           p["ln2_g"][None, :], p["ln2_b"][None, :], alpha, batch, seq, mem_len,
                           tiles["mem_tm"])

    dff = p["w_down"].shape[0]
    tn = tiles["ffn_tn"]
    ffp = -(-dff // tn) * tn
    w_up = p["w_up"]
    cw = p["ffn_conv_w"]
    cb = p["ffn_conv_b"][None, :]
    x3 = _ffn_ln(x2,
                 _pad_cols(w_up[:, :dff], ffp).astype(BF16), _pad_cols(w_up[:, dff:], ffp).astype(BF16),
                 _pad_cols(cw[:, :dff], ffp), _pad_cols(cw[:, dff:], ffp),
                 _pad_cols(cb[:, :dff], ffp), _pad_cols(cb[:, dff:], ffp),
                 jnp.pad(p["w_down"], ((0, ffp - dff), (0, 0))).astype(BF16),
                 p["ln3_g"][None, :], p["ln3_b"][None, :], alpha, seq, tiles["ffn_tm"], tn)
    return x3


_TILES = dict(proj_tm=1024, proj_tn=768, merge_tm=512, merge_tn=512, ln_tm=512, kv_tn=512,
              mem_tm=512, ffn_tm=512, ffn_tn=512)

_PER_LAYER = ("w_in", "swa_sinks", "gdn_conv_w", "gdn_a_log", "gdn_dt_bias", "gdn_norm_w", "w_br_swa",
              "w_br_gdn", "w_mix_o", "ln1_g", "ln1_b", "w_mem_q", "w_mem_kv", "w_mem_o", "ln2_g", "ln2_b",
              "w_up", "ffn_conv_w", "ffn_conv_b", "w_down", "ln3_g", "ln3_b")


def _forward(x, mem, rel_bias, weights, tiles):
    batch, seq, d = x.shape
    mem_len = mem.shape[1]
    depth = weights["w_in"].shape[0]
    alpha = (2 * depth) ** 0.25
    bias_tab = _swa_bias_table(rel_bias)
    x2d = x.reshape(batch * seq, d)
    mem2d = mem.reshape(batch * mem_len, d)
    for l in range(depth):
        p = {name: weights[name][l] for name in _PER_LAYER}
        x2d = _layer(x2d, mem2d, batch, seq, mem_len, p, bias_tab, alpha, tiles)
    return x2d.reshape(batch, seq, d)


def kernel(x, mem, w_in, rel_bias, swa_sinks, gdn_conv_w, gdn_a_log, gdn_dt_bias, gdn_norm_w, w_br_swa, w_br_gdn, w_mix_o, ln1_g, ln1_b, w_mem_q, w_mem_kv, w_mem_o, ln2_g, ln2_b, w_up, ffn_conv_w, ffn_conv_b, w_down, ln3_g, ln3_b):
    weights = dict(w_in=w_in, swa_sinks=swa_sinks, gdn_conv_w=gdn_conv_w, gdn_a_log=gdn_a_log,
                   gdn_dt_bias=gdn_dt_bias, gdn_norm_w=gdn_norm_w, w_br_swa=w_br_swa, w_br_gdn=w_br_gdn,
                   w_mix_o=w_mix_o, ln1_g=ln1_g, ln1_b=ln1_b, w_mem_q=w_mem_q, w_mem_kv=w_mem_kv,
                   w_mem_o=w_mem_o, ln2_g=ln2_g, ln2_b=ln2_b, w_up=w_up, ffn_conv_w=ffn_conv_w,
                   ffn_conv_b=ffn_conv_b, w_down=w_down, ln3_g=ln3_g, ln3_b=ln3_b)
    return _forward(x, mem, rel_bias, weights, _TILES)
```

```python
import functools
import math

import jax
import jax.numpy as jnp
from jax import lax
from jax.experimental import pallas as pl
from jax.experimental.pallas import tpu as pltpu

F32 = jnp.float32
BF16 = jnp.bfloat16

LANES = 128
BF16_ROWS = 16
VMEM_LIMIT = 56 * 1024 * 1024

SWA_HEADS = 16
SWA_KV_HEADS = 2
SWA_HEAD_DIM = 64
SWA_BLOCK = 128
REL_BUCKETS = 32
REL_MAX_DIST = 128
GDN_HEADS = 8
GDN_HEAD_DIM = 128
GDN_CONV = 4
GDN_CHUNK = 64
GDN_PAIR = 2 * GDN_CHUNK
MEM_HEADS = 4
MEM_HEAD_DIM = 128
FFN_CONV = 3
NORM_EPS = 1e-5
NEG_INF = -1e30


def _params(*sem):
    return pltpu.CompilerParams(dimension_semantics=sem, vmem_limit_bytes=VMEM_LIMIT)


def _dot(a, b):
    return jnp.dot(a.astype(BF16), b.astype(BF16), preferred_element_type=F32)


def _dot_nt(a, b):
    return lax.dot_general(a.astype(BF16), b.astype(BF16), (((1,), (1,)), ((), ())),
                           preferred_element_type=F32)


def _split(a):
    hi = a.astype(BF16)
    lo = (a - hi.astype(F32)).astype(BF16)
    return hi, lo


def _dot3(a, b):
    ah, al = _split(a)
    bh, bl = _split(b)
    d = functools.partial(jnp.dot, preferred_element_type=F32)
    return d(ah, bh) + (d(al, bh) + d(ah, bl))


def _sigmoid(x):
    return 1.0 / (1.0 + jnp.exp(-x))


def _silu(x):
    return x * _sigmoid(x)


def _layer_norm(y, g, b):
    mu = jnp.mean(y, axis=-1, keepdims=True)
    yc = y - mu
    var = jnp.mean(yc * yc, axis=-1, keepdims=True)
    return yc * lax.rsqrt(var + NORM_EPS) * g + b


def _mm_kernel(x_ref, w_ref, o_ref):
    o_ref[...] = jnp.dot(x_ref[...], w_ref[...], preferred_element_type=F32).astype(o_ref.dtype)


def _matmul(x, w, tm, tn, out_dtype):
    m, k = x.shape
    n = w.shape[1]
    assert m % tm == 0 and n % tn == 0
    return pl.pallas_call(
        _mm_kernel,
        grid=(m // tm, n // tn),
        in_specs=[pl.BlockSpec((tm, k), lambda i, j: (i, 0)),
                  pl.BlockSpec((k, tn), lambda i, j: (0, j))],
        out_specs=pl.BlockSpec((tm, tn), lambda i, j: (i, j)),
        out_shape=jax.ShapeDtypeStruct((m, n), out_dtype),
        compiler_params=_params("parallel", "arbitrary"),
        name="matmul",
    )(x, w)


def _t5_causal_bucket(dist):
    max_exact = REL_BUCKETS // 2
    d = jnp.maximum(dist, 1).astype(F32)
    large = max_exact + (jnp.log(d / max_exact) / math.log(REL_MAX_DIST / max_exact)
                         * (REL_BUCKETS - max_exact)).astype(jnp.int32)
    large = jnp.minimum(large, REL_BUCKETS - 1)
    return jnp.where(dist < max_exact, dist, large)


def _swa_bias_table(rel_bias):
    qi = jnp.arange(SWA_BLOCK)[:, None]
    kj = jnp.arange(2 * SWA_BLOCK)[None, :]
    dist = qi + SWA_BLOCK - kj
    in_window = (dist >= 0) & (dist < SWA_BLOCK)
    bias = rel_bias.astype(F32)[_t5_causal_bucket(jnp.maximum(dist, 0))]
    bias = bias.transpose(2, 0, 1)
    return jnp.where(in_window[None], bias, NEG_INF)


def _swa_kernel(sink_ref, q_ref, kv_ref, pkv_ref, bias_ref, o_ref, *, blocks_per_batch):
    half = SWA_HEAD_DIM
    first = (pl.program_id(0) % blocks_per_batch) == 0
    kv = jnp.concatenate([pkv_ref[...], kv_ref[...]], axis=0)
    k = kv[:, :LANES]
    v = kv[:, LANES:]
    lo = lax.broadcasted_iota(jnp.int32, k.shape, 1) < half
    zero = jnp.zeros_like(k)

    def swap(t):
        return jnp.concatenate([t[:, half:], t[:, :half]], axis=1)

    def padded(t):
        tr = swap(t)
        return ((jnp.where(lo, t, zero), jnp.where(lo, zero, tr)),
                (jnp.where(lo, tr, zero), jnp.where(lo, zero, t)))

    kpad = padded(k)
    vpad = padded(v)
    col = lax.broadcasted_iota(jnp.int32, (SWA_BLOCK, 2 * SWA_BLOCK), 1)
    no_prev = jnp.logical_and(first, col < SWA_BLOCK)
    out_lo = lax.broadcasted_iota(jnp.int32, (SWA_BLOCK, LANES), 1) < half
    pairs = SWA_HEADS // 2
    for p in range(pairs):
        kvh = (2 * p) // (SWA_HEADS // SWA_KV_HEADS)
        qc = q_ref[:, p * LANES:(p + 1) * LANES]
        acc = None
        inv = []
        for par in range(2):
            head = 2 * p + par
            s = _dot_nt(qc, kpad[kvh][par]) * (SWA_HEAD_DIM ** -0.5) + bias_ref[head]
            s = jnp.where(no_prev, NEG_INF, s)
            sink = sink_ref[head]
            m = jnp.maximum(jnp.max(s, axis=-1, keepdims=True), sink)
            e = jnp.exp(s - m)
            d = jnp.sum(e, axis=-1, keepdims=True) + jnp.exp(sink - m)
            pv = _dot(e, vpad[kvh][par])
            acc = pv if acc is None else acc + pv
            inv.append(1.0 / d)
        o_ref[:, p * LANES:(p + 1) * LANES] = (acc * jnp.where(out_lo, inv[0], inv[1])).astype(o_ref.dtype)


def _swa_attention(proj, q_col, kv_col, sinks, bias_tab, seq):
    t = proj.shape[0]
    qw = SWA_HEADS * SWA_HEAD_DIM
    kvw = 2 * SWA_KV_HEADS * SWA_HEAD_DIM
    bpb = seq // SWA_BLOCK
    return pl.pallas_call(
        functools.partial(_swa_kernel, blocks_per_batch=bpb),
        grid=(t // SWA_BLOCK,),
        in_specs=[pl.BlockSpec(memory_space=pltpu.SMEM),
                  pl.BlockSpec((SWA_BLOCK, qw), lambda i: (i, q_col)),
                  pl.BlockSpec((SWA_BLOCK, kvw), lambda i: (i, kv_col)),
                  pl.BlockSpec((SWA_BLOCK, kvw), lambda i: (jnp.maximum(i - 1, 0), kv_col)),
                  pl.BlockSpec((SWA_HEADS, SWA_BLOCK, 2 * SWA_BLOCK), lambda i: (0, 0, 0))],
        out_specs=pl.BlockSpec((SWA_BLOCK, qw), lambda i: (i, 0)),
        out_shape=jax.ShapeDtypeStruct((t, qw), BF16),
        compiler_params=_params("parallel"),
        name="swa_attention",
    )(sinks, proj, proj, proj, bias_tab)


def _gdn_pre_kernel(q_ref, k_ref, v_ref, hq_ref, hk_ref, hv_ref, ba_ref, cw_ref, alog_ref, dtb_ref,
                    u_ref, w_ref, qd_ref, ai_ref, kdt_ref, gl_ref, *, pairs_per_batch):
    n = GDN_PAIR
    c = GDN_CHUNK
    first = (pl.program_id(0) % pairs_per_batch) == 0
    row = lax.broadcasted_iota(jnp.int32, (n, n), 0)
    col = lax.broadcasted_iota(jnp.int32, (n, n), 1)
    same = (row >= c) == (col >= c)
    tril = jnp.logical_and(same, row >= col)
    strict = jnp.logical_and(same, row > col)
    eye = jnp.where(row == col, 1.0, 0.0).astype(F32)
    rowc = lax.broadcasted_iota(jnp.int32, (n, 1), 0)

    ba = ba_ref[...]
    beta_all = _sigmoid(ba)
    x = ba + dtb_ref[...]
    softplus = jnp.maximum(x, 0.0) + jnp.log1p(jnp.exp(-jnp.abs(x)))
    g_all = -jnp.exp(alog_ref[...]) * softplus
    gc_all = _dot3(jnp.where(tril, 1.0, 0.0).astype(F32), g_all)
    gct_all = gc_all.T

    def conv_silu(cur_ref, halo_ref, which, sl):
        cur = cur_ref[:, sl].astype(F32)
        halo = jnp.where(first, 0.0, halo_ref[:, sl].astype(F32))
        ext = jnp.concatenate([halo, cur], axis=0)
        base = which * GDN_HEADS * GDN_HEAD_DIM
        cw = cw_ref[:, base + sl.start:base + sl.stop]
        y = cw[GDN_CONV - 1:GDN_CONV] * cur
        for j in range(1, GDN_CONV):
            y = y + cw[GDN_CONV - 1 - j:GDN_CONV - j] * pltpu.roll(ext, j, axis=0)[BF16_ROWS:]
        return _silu(y)

    def l2norm(t):
        return t * lax.rsqrt(jnp.sum(t * t, axis=-1, keepdims=True) + 1e-6)

    heads = range(GDN_HEADS)
    sls = [slice(h * LANES, (h + 1) * LANES) for h in heads]
    dotf = functools.partial(jnp.dot, preferred_element_type=F32)

    a_mats, rhs = [], []
    for h in heads:
        sl = sls[h]
        q = l2norm(conv_silu(q_ref, hq_ref, 0, sl)) * (GDN_HEAD_DIM ** -0.5)
        k = l2norm(conv_silu(k_ref, hk_ref, 1, sl))
        v = conv_silu(v_ref, hv_ref, 2, sl)
        beta = beta_all[:, h:h + 1]
        gc = gc_all[:, GDN_HEADS + h:GDN_HEADS + h + 1]
        gcr = gct_all[GDN_HEADS + h:GDN_HEADS + h + 1, :]
        kb = k * beta
        kq = _dot_nt(jnp.concatenate([kb, q], axis=0), k)
        decay = jnp.where(tril, jnp.exp(jnp.where(tril, gc - gcr, 0.0)), 0.0)
        a_mats.append(jnp.where(strict, kq[:n] * decay, 0.0))
        egc = jnp.exp(gc)
        rhs.append(jnp.concatenate([v * beta, kb * egc], axis=1))
        gl_e = gc[c - 1:c, :]
        gl_o = gc[n - 1:n, :]
        kd = k * jnp.exp(jnp.where(rowc < c, gl_e, gl_o) - gc)
        qd_ref[:, sl] = (q * egc).astype(qd_ref.dtype)
        ai_ref[:, sl] = (kq[n:] * decay).astype(ai_ref.dtype)
        kdt_ref[:, sl] = kd.T.astype(kdt_ref.dtype)
        gl_ref[0, 0:1, sl] = jnp.broadcast_to(jnp.exp(gl_e), (1, LANES))
        gl_ref[0, 1:2, sl] = jnp.broadcast_to(jnp.exp(gl_o), (1, LANES))

    ps = [eye - a for a in a_mats]
    aks = []
    for a in a_mats:
        ah, al = _split(a)
        aks.append(dotf(ah, ah) + (dotf(al, ah) + dotf(ah, al)))
    for _ in range(4):
        for h in heads:
            akh, akl = _split(aks[h])
            ph, pl_ = _split(ps[h])
            lh = jnp.concatenate([akh, ph], axis=0)
            ll = jnp.concatenate([akl, pl_], axis=0)
            m = dotf(lh, akh) + (dotf(ll, akh) + dotf(lh, akl))
            aks[h] = m[:n]
            ps[h] = ps[h] + m[n:]
    for h in heads:
        ps[h] = ps[h] + _dot3(ps[h], aks[h])
    for h in heads:
        sol = _dot3(ps[h], rhs[h])
        u_ref[:, sls[h]] = sol[:, :LANES]
        w_ref[:, sls[h]] = sol[:, LANES:].astype(w_ref.dtype)


def _gdn_prepass(proj, ba, conv_w, alog_row, dtb_row, seq):
    t = proj.shape[0]
    gw = GDN_HEADS * GDN_HEAD_DIM
    n = GDN_PAIR
    ppb = seq // n
    hb = n // BF16_ROWS

    def cur(cb):
        return pl.BlockSpec((n, gw), lambda i: (i, cb))

    def halo(cb):
        return pl.BlockSpec((BF16_ROWS, gw), lambda i: (jnp.maximum(i * hb - 1, 0), cb))

    row_spec = pl.BlockSpec((n, gw), lambda i: (i, 0))
    return pl.pallas_call(
        functools.partial(_gdn_pre_kernel, pairs_per_batch=ppb),
        grid=(t // n,),
        in_specs=[cur(0), cur(1), cur(2), halo(0), halo(1), halo(2),
                  pl.BlockSpec((n, LANES), lambda i: (i, 0)),
                  pl.BlockSpec((GDN_CONV, 3 * gw), lambda i: (0, 0)),
                  pl.BlockSpec((1, LANES), lambda i: (0, 0)),
                  pl.BlockSpec((1, LANES), lambda i: (0, 0))],
        out_specs=[row_spec, row_spec, row_spec, row_spec, row_spec,
                   pl.BlockSpec((1, 2, gw), lambda i: (i, 0, 0))],
        out_shape=[jax.ShapeDtypeStruct((t, gw), F32),
                   jax.ShapeDtypeStruct((t, gw), BF16),
                   jax.ShapeDtypeStruct((t, gw), BF16),
                   jax.ShapeDtypeStruct((t, gw), BF16),
                   jax.ShapeDtypeStruct((t, gw), BF16),
                   jax.ShapeDtypeStruct((t // n, 2, gw), F32)],
        compiler_params=_params("parallel"),
        name="gdn_prepass",
    )(proj, proj, proj, proj, proj, proj, ba, conv_w, alog_row, dtb_row)


def _gdn_scan_kernel(u_ref, w_ref, qd_ref, ai_ref, kdt_ref, gl_ref, z_ref, nw_ref, o_ref, st_ref):
    c = GDN_CHUNK

    @pl.when(pl.program_id(1) == 0)
    def _():
        st_ref[...] = jnp.zeros_like(st_ref)

    zeros = jnp.zeros((c, LANES), F32)
    nw = nw_ref[...]
    heads = range(GDN_HEADS)
    sls = [slice(h * LANES, (h + 1) * LANES) for h in heads]
    states = [st_ref[h] for h in heads]
    for ci in range(2):
        rs = slice(ci * c, (ci + 1) * c)
        sbs = [s.astype(BF16) for s in states]
        ws = [_dot(w_ref[rs, sls[h]], sbs[h]) for h in heads]
        qs = [_dot(qd_ref[rs, sls[h]], sbs[h]) for h in heads]
        vpads = []
        for h in heads:
            vnew = u_ref[rs, sls[h]] - ws[h]
            vpads.append(jnp.concatenate([vnew, zeros] if ci == 0 else [zeros, vnew], axis=0).astype(BF16))
        outs = [qs[h] + _dot(ai_ref[rs, sls[h]], vpads[h]) for h in heads]
        states = [states[h] * gl_ref[0, ci:ci + 1, sls[h]] + _dot(kdt_ref[:, sls[h]], vpads[h]) for h in heads]
        for h in heads:
            o = outs[h]
            z = z_ref[rs, sls[h]].astype(F32)
            o = o * lax.rsqrt(jnp.mean(o * o, axis=-1, keepdims=True) + 1e-6) * nw * _silu(z)
            o_ref[rs, sls[h]] = o.astype(o_ref.dtype)
    for h in heads:
        st_ref[h] = states[h]


def _gdn_scan(u, w, qd, ai, kdt, gl, proj, z_col, norm_w_row, batch, seq):
    t = u.shape[0]
    gw = GDN_HEADS * GDN_HEAD_DIM
    n = GDN_PAIR
    ppb = seq // n
    row_spec = pl.BlockSpec((n, gw), lambda b, i: (b * ppb + i, 0))
    return pl.pallas_call(
        _gdn_scan_kernel,
        grid=(batch, ppb),
        in_specs=[row_spec, row_spec, row_spec, row_spec, row_spec,
                  pl.BlockSpec((1, 2, gw), lambda b, i: (b * ppb + i, 0, 0)),
                  pl.BlockSpec((n, gw), lambda b, i: (b * ppb + i, z_col)),
                  pl.BlockSpec((1, LANES), lambda b, i: (0, 0))],
        out_specs=row_spec,
        out_shape=jax.ShapeDtypeStruct((t, gw), BF16),
        scratch_shapes=[pltpu.VMEM((GDN_HEADS, GDN_HEAD_DIM, GDN_HEAD_DIM), F32)],
        compiler_params=_params("arbitrary", "arbitrary"),
        name="gdn_scan",
    )(u, w, qd, ai, kdt, gl, proj, norm_w_row)


def _merge_kernel(x_ref, a_ref, g_ref, wgs_ref, wgg_ref, wbs_ref, wbg_ref, o_ref):
    x = x_ref[...]
    gate_s = _sigmoid(jnp.dot(x, wgs_ref[...], preferred_element_type=F32))
    gate_g = _sigmoid(jnp.dot(x, wgg_ref[...], preferred_element_type=F32))
    y_s = jnp.dot(a_ref[...], wbs_ref[...], preferred_element_type=F32)
    y_g = jnp.dot(g_ref[...], wbg_ref[...], preferred_element_type=F32)
    o_ref[...] = (gate_s * y_s + gate_g * y_g).astype(o_ref.dtype)


def _merge(xb, attn, gdn, wgs, wgg, wbs, wbg, tm, tn):
    t, d = xb.shape
    ws = attn.shape[1]
    wg = gdn.shape[1]
    return pl.pallas_call(
        _merge_kernel,
        grid=(t // tm, d // tn),
        in_specs=[pl.BlockSpec((tm, d), lambda i, j: (i, 0)),
                  pl.BlockSpec((tm, ws), lambda i, j: (i, 0)),
                  pl.BlockSpec((tm, wg), lambda i, j: (i, 0)),
                  pl.BlockSpec((d, tn), lambda i, j: (0, j)),
                  pl.BlockSpec((d, tn), lambda i, j: (0, j)),
                  pl.BlockSpec((ws, tn), lambda i, j: (0, j)),
                  pl.BlockSpec((wg, tn), lambda i, j: (0, j))],
        out_specs=pl.BlockSpec((tm, tn), lambda i, j: (i, j)),
        out_shape=jax.ShapeDtypeStruct((t, d), BF16),
        compiler_params=_params("parallel", "arbitrary"),
        name="branch_merge",
    )(xb, attn, gdn, wgs, wgg, wbs, wbg)


def _mixo_ln_kernel(m_ref, x_ref, w_ref, g_ref, b_ref, o_ref, *, alpha):
    y = alpha * x_ref[...] + jnp.dot(m_ref[...], w_ref[...], preferred_element_type=F32)
    o_ref[...] = _layer_norm(y, g_ref[...], b_ref[...])


def _mixo_ln(mixed, x, w, g, b, alpha, tm):
    t, d = x.shape
    vec = pl.BlockSpec((1, d), lambda i: (0, 0))
    return pl.pallas_call(
        functools.partial(_mixo_ln_kernel, alpha=alpha),
        grid=(t // tm,),
        in_specs=[pl.BlockSpec((tm, d), lambda i: (i, 0)),
                  pl.BlockSpec((tm, d), lambda i: (i, 0)),
                  pl.BlockSpec((d, d), lambda i: (0, 0)),
                  vec, vec],
        out_specs=pl.BlockSpec((tm, d), lambda i: (i, 0)),
        out_shape=jax.ShapeDtypeStruct((t, d), F32),
        compiler_params=_params("parallel"),
        name="mixo_layernorm",
    )(mixed, x, w, g, b)


def _mem_kernel(x_ref, wq_ref, kv_ref, wo_ref, g_ref, b_ref, o_ref, *, alpha):
    x = x_ref[...]
    q = jnp.dot(x.astype(BF16), wq_ref[...], preferred_element_type=F32)
    mw = MEM_HEADS * MEM_HEAD_DIM
    outs = []
    for h in range(MEM_HEADS):
        sl = slice(h * MEM_HEAD_DIM, (h + 1) * MEM_HEAD_DIM)
        s = _dot_nt(q[:, sl], kv_ref[:, sl]) * (MEM_HEAD_DIM ** -0.5)
        m = jnp.max(s, axis=-1, keepdims=True)
        e = jnp.exp(s - m)
        d = jnp.sum(e, axis=-1, keepdims=True)
        vh = kv_ref[:, mw + h * MEM_HEAD_DIM:mw + (h + 1) * MEM_HEAD_DIM]
        outs.append(_dot(e, vh) / d)
    o = jnp.concatenate(outs, axis=1)
    y = alpha * x + _dot(o, wo_ref[...])
    o_ref[...] = _layer_norm(y, g_ref[...], b_ref[...])


def _mem_attention_ln(x, wq, kv, wo, g, b, alpha, batch, seq, mem_len, tm):
    t, d = x.shape
    mw = MEM_HEADS * MEM_HEAD_DIM
    tpb = seq // tm
    vec = pl.BlockSpec((1, d), lambda bi, i: (0, 0))
    return pl.pallas_call(
        functools.partial(_mem_kernel, alpha=alpha),
        grid=(batch, tpb),
        in_specs=[pl.BlockSpec((tm, d), lambda bi, i: (bi * tpb + i, 0)),
                  pl.BlockSpec((d, mw), lambda bi, i: (0, 0)),
                  pl.BlockSpec((mem_len, 2 * mw), lambda bi, i: (bi, 0)),
                  pl.BlockSpec((mw, d), lambda bi, i: (0, 0)),
                  vec, vec],
        out_specs=pl.BlockSpec((tm, d), lambda bi, i: (bi * tpb + i, 0)),
        out_shape=jax.ShapeDtypeStruct((t, d), F32),
        compiler_params=_params("parallel", "arbitrary"),
        name="mem_attention_layernorm",
    )(x, wq, kv, wo, g, b)


def _ffn_kernel(x_ref, halo_ref, wg_ref, wu_ref, cg_ref, cu_ref, bg_ref, bu_ref, wd_ref, g_ref, b_ref,
                o_ref, xb_ref, acc_ref, *, alpha, tiles_per_batch):
    j = pl.program_id(1)
    hr = BF16_ROWS

    @pl.when(j == 0)
    def _():
        first = (pl.program_id(0) % tiles_per_batch) == 0
        xb_ref[:hr, :] = jnp.where(first, 0.0, halo_ref[...]).astype(BF16)
        xb_ref[hr:, :] = x_ref[...].astype(BF16)
        acc_ref[...] = jnp.zeros_like(acc_ref)

    xb = xb_ref[...]

    def conv(w_ref, cw_ref, bias_ref):
        h = jnp.dot(xb, w_ref[...], preferred_element_type=F32)
        cw = cw_ref[...]
        y = cw[FFN_CONV - 1:FFN_CONV] * h[hr:] + bias_ref[...]
        for k in range(1, FFN_CONV):
            y = y + cw[FFN_CONV - 1 - k:FFN_CONV - k] * pltpu.roll(h, k, axis=0)[hr:]
        return y

    act = _silu(conv(wg_ref, cg_ref, bg_ref)) * conv(wu_ref, cu_ref, bu_ref)
    acc_ref[...] += _dot(act, wd_ref[...])

    @pl.when(j == pl.num_programs(1) - 1)
    def _():
        y = alpha * x_ref[...] + acc_ref[...]
        o_ref[...] = _layer_norm(y, g_ref[...], b_ref[...])


def _ffn_ln(x, wg, wu, cg, cu, bg, bu, wd, g, b, alpha, seq, tm, tn):
    t, d = x.shape
    ff = wg.shape[1]
    tpb = seq // tm
    hb = tm // BF16_ROWS
    vec = pl.BlockSpec((1, d), lambda i, j: (0, 0))
    return pl.pallas_call(
        functools.partial(_ffn_kernel, alpha=alpha, tiles_per_batch=tpb),
        grid=(t // tm, ff // tn),
        in_specs=[pl.BlockSpec((tm, d), lambda i, j: (i, 0)),
                  pl.BlockSpec((BF16_ROWS, d), lambda i, j: (jnp.maximum(i * hb - 1, 0), 0)),
                  pl.BlockSpec((d, tn), lambda i, j: (0, j)),
                  pl.BlockSpec((d, tn), lambda i, j: (0, j)),
                  pl.BlockSpec((FFN_CONV, tn), lambda i, j: (0, j)),
                  pl.BlockSpec((FFN_CONV, tn), lambda i, j: (0, j)),
                  pl.BlockSpec((1, tn), lambda i, j: (0, j)),
                  pl.BlockSpec((1, tn), lambda i, j: (0, j)),
                  pl.BlockSpec((tn, d), lambda i, j: (j, 0)),
                  vec, vec],
        out_specs=pl.BlockSpec((tm, d), lambda i, j: (i, 0)),
        out_shape=jax.ShapeDtypeStruct((t, d), F32),
        scratch_shapes=[pltpu.VMEM((BF16_ROWS + tm, d), BF16),
                        pltpu.VMEM((tm, d), F32)],
        compiler_params=_params("parallel", "arbitrary"),
        name="ffn_layernorm",
    )(x, x, wg, wu, cg, cu, bg, bu, wd, g, b)


def _pad_cols(a, n):
    return jnp.pad(a, ((0, 0), (0, n - a.shape[1])))


def _layer(x2d, mem2d, batch, seq, mem_len, p, bias_tab, alpha, tiles):
    d = x2d.shape[1]
    sq = SWA_HEADS * SWA_HEAD_DIM
    skv = SWA_KV_HEADS * SWA_HEAD_DIM
    gw = GDN_HEADS * GDN_HEAD_DIM
    w_in = p["w_in"]
    o_gdn = sq + 2 * skv
    o_ba = o_gdn + 4 * gw
    o_gate = o_ba + 2 * GDN_HEADS
    w_proj = jnp.concatenate([w_in[:, o_gdn:o_ba], w_in[:, :o_gdn]], axis=1).astype(BF16)
    w_ba = _pad_cols(w_in[:, o_ba:o_gate], LANES).astype(BF16)
    w_gs = w_in[:, o_gate:o_gate + d].astype(BF16)
    w_gg = w_in[:, o_gate + d:o_gate + 2 * d].astype(BF16)

    xb = x2d.astype(BF16)
    proj = _matmul(xb, w_proj, tiles["proj_tm"], tiles["proj_tn"], BF16)
    ba = _matmul(xb, w_ba, tiles["proj_tm"], LANES, F32)

    attn = _swa_attention(proj, (4 * gw) // sq, (4 * gw + sq) // (2 * skv), p["swa_sinks"], bias_tab, seq)

    zpad = jnp.zeros((GDN_HEADS,), F32)
    alog_row = _pad_cols(jnp.concatenate([zpad, p["gdn_a_log"]])[None, :], LANES)
    dtb_row = _pad_cols(jnp.concatenate([zpad, p["gdn_dt_bias"]])[None, :], LANES)
    u, w, qd, ai, kdt, gl = _gdn_prepass(proj, ba, p["gdn_conv_w"], alog_row, dtb_row, seq)
    gdn = _gdn_scan(u, w, qd, ai, kdt, gl, proj, 3, p["gdn_norm_w"][None, :], batch, seq)

    mixed = _merge(xb, attn, gdn, w_gs, w_gg, p["w_br_swa"].astype(BF16), p["w_br_gdn"].astype(BF16),
                   tiles["merge_tm"], tiles["merge_tn"])
    x1 = _mixo_ln(mixed, x2d, p["w_mix_o"].astype(BF16), p["ln1_g"][None, :], p["ln1_b"][None, :],
                  alpha, tiles["ln_tm"])

    kv = _matmul(mem2d.astype(BF16), p["w_mem_kv"].astype(BF16), mem_len, tiles["kv_tn"], BF16)
    x2 = _mem_attention_ln(x1, p["w_mem_q"].astype(BF16), kv, p["w_mem_o"].astype(BF16),
                           p["ln2_g"][None, :], p["ln2_b"][None, :], alpha, batch, seq, mem_len,
                           tiles["mem_tm"])

    dff = p["w_down"].shape[0]
    tn = tiles["ffn_tn"]
    ffp = -(-dff // tn) * tn
    w_up = p["w_up"]
    cw = p["ffn_conv_w"]
    cb = p["ffn_conv_b"][None, :]
    x3 = _ffn_ln(x2,
                 _pad_cols(w_up[:, :dff], ffp).astype(BF16), _pad_cols(w_up[:, dff:], ffp).astype(BF16),
                 _pad_cols(cw[:, :dff], ffp), _pad_cols(cw[:, dff:], ffp),
                 _pad_cols(cb[:, :dff], ffp), _pad_cols(cb[:, dff:], ffp),
                 jnp.pad(p["w_down"], ((0, ffp - dff), (0, 0))).astype(BF16),
                 p["ln3_g"][None, :], p["ln3_b"][None, :], alpha, seq, tiles["ffn_tm"], tn)
    return x3


_TILES = dict(proj_tm=1024, proj_tn=768, merge_tm=512, merge_tn=512, ln_tm=512, kv_tn=512,
              mem_tm=512, ffn_tm=512, ffn_tn=512)

_PER_LAYER = ("w_in", "swa_sinks", "gdn_conv_w", "gdn_a_log", "gdn_dt_bias", "gdn_norm_w", "w_br_swa",
              "w_br_gdn", "w_mix_o", "ln1_g", "ln1_b", "w_mem_q", "w_mem_kv", "w_mem_o", "ln2_g", "ln2_b",
              "w_up", "ffn_conv_w", "ffn_conv_b", "w_down", "ln3_g", "ln3_b")


def _forward(x, mem, rel_bias, weights, tiles):
    batch, seq, d = x.shape
    mem_len = mem.shape[1]
    depth = weights["w_in"].shape[0]
    alpha = (2 * depth) ** 0.25
    bias_tab = _swa_bias_table(rel_bias)
    x2d = x.reshape(batch * seq, d)
    mem2d = mem.reshape(batch * mem_len, d)
    for l in range(depth):
        p = {name: weights[name][l] for name in _PER_LAYER}
        x2d = _layer(x2d, mem2d, batch, seq, mem_len, p, bias_tab, alpha, tiles)
    return x2d.reshape(batch, seq, d)


def kernel(x, mem, w_in, rel_bias, swa_sinks, gdn_conv_w, gdn_a_log, gdn_dt_bias, gdn_norm_w, w_br_swa, w_br_gdn, w_mix_o, ln1_g, ln1_b, w_mem_q, w_mem_kv, w_mem_o, ln2_g, ln2_b, w_up, ffn_conv_w, ffn_conv_b, w_down, ln3_g, ln3_b):
    weights = dict(w_in=w_in, swa_sinks=swa_sinks, gdn_conv_w=gdn_conv_w, gdn_a_log=gdn_a_log,
                   gdn_dt_bias=gdn_dt_bias, gdn_norm_w=gdn_norm_w, w_br_swa=w_br_swa, w_br_gdn=w_br_gdn,
                   w_mix_o=w_mix_o, ln1_g=ln1_g, ln1_b=ln1_b, w_mem_q=w_mem_q, w_mem_kv=w_mem_kv,
                   w_mem_o=w_mem_o, ln2_g=ln2_g, ln2_b=ln2_b, w_up=w_up, ffn_conv_w=ffn_conv_w,
                   ffn_conv_b=ffn_conv_b, w_down=w_down, ln3_g=ln3_g, ln3_b=ln3_b)
    return _forward(x, mem, rel_bias, weights, _TILES)
```

```python
import functools
import math

import jax
import jax.numpy as jnp
from jax import lax
from jax.experimental import pallas as pl
from jax.experimental.pallas import tpu as pltpu

F32 = jnp.float32
BF16 = jnp.bfloat16

LANES = 128
BF16_ROWS = 16
VMEM_LIMIT = 56 * 1024 * 1024

SWA_HEADS = 16
SWA_KV_HEADS = 2
SWA_HEAD_DIM = 64
SWA_BLOCK = 128
REL_BUCKETS = 32
REL_MAX_DIST = 128
GDN_HEADS = 8
GDN_HEAD_DIM = 128
GDN_CONV = 4
GDN_CHUNK = 64
GDN_PAIR = 2 * GDN_CHUNK
MEM_HEADS = 4
MEM_HEAD_DIM = 128
FFN_CONV = 3
NORM_EPS = 1e-5
NEG_INF = -1e30


def _params(*sem):
    return pltpu.CompilerParams(dimension_semantics=sem, vmem_limit_bytes=VMEM_LIMIT)


def _dot(a, b):
    return jnp.dot(a.astype(BF16), b.astype(BF16), preferred_element_type=F32)


def _dot_nt(a, b):
    return lax.dot_general(a.astype(BF16), b.astype(BF16), (((1,), (1,)), ((), ())),
                           preferred_element_type=F32)


def _split(a):
    hi = a.astype(BF16)
    lo = (a - hi.astype(F32)).astype(BF16)
    return hi, lo


def _dot3(a, b):
    ah, al = _split(a)
    bh, bl = _split(b)
    d = functools.partial(jnp.dot, preferred_element_type=F32)
    return d(ah, bh) + (d(al, bh) + d(ah, bl))


def _sigmoid(x):
    return 1.0 / (1.0 + jnp.exp(-x))


def _silu(x):
    return x * _sigmoid(x)


def _layer_norm(y, g, b):
    mu = jnp.mean(y, axis=-1, keepdims=True)
    yc = y - mu
    var = jnp.mean(yc * yc, axis=-1, keepdims=True)
    return yc * lax.rsqrt(var + NORM_EPS) * g + b


def _mm_kernel(x_ref, w_ref, o_ref):
    o_ref[...] = jnp.dot(x_ref[...], w_ref[...], preferred_element_type=F32).astype(o_ref.dtype)


def _matmul(x, w, tm, tn, out_dtype):
    m, k = x.shape
    n = w.shape[1]
    assert m % tm == 0 and n % tn == 0
    return pl.pallas_call(
        _mm_kernel,
        grid=(m // tm, n // tn),
        in_specs=[pl.BlockSpec((tm, k), lambda i, j: (i, 0)),
                  pl.BlockSpec((k, tn), lambda i, j: (0, j))],
        out_specs=pl.BlockSpec((tm, tn), lambda i, j: (i, j)),
        out_shape=jax.ShapeDtypeStruct((m, n), out_dtype),
        compiler_params=_params("parallel", "arbitrary"),
        name="matmul",
    )(x, w)


def _proj_kernel(x_ref, w_ref, o_ref, xb_ref):
    @pl.when(pl.program_id(1) == 0)
    def _():
        xb_ref[...] = x_ref[...].astype(xb_ref.dtype)

    o_ref[...] = jnp.dot(xb_ref[...], w_ref[...], preferred_element_type=F32).astype(o_ref.dtype)


def _proj(x, w, tm, tn):
    m, k = x.shape
    n = w.shape[1]
    assert m % tm == 0 and n % tn == 0
    return pl.pallas_call(
        _proj_kernel,
        grid=(m // tm, n // tn),
        in_specs=[pl.BlockSpec((tm, k), lambda i, j: (i, 0)),
                  pl.BlockSpec((k, tn), lambda i, j: (0, j))],
        out_specs=[pl.BlockSpec((tm, tn), lambda i, j: (i, j)),
                   pl.BlockSpec((tm, k), lambda i, j: (i, 0))],
        out_shape=[jax.ShapeDtypeStruct((m, n), BF16), jax.ShapeDtypeStruct((m, k), BF16)],
        compiler_params=_params("parallel", "arbitrary"),
        name="input_proj",
    )(x, w)


def _t5_causal_bucket(dist):
    max_exact = REL_BUCKETS // 2
    d = jnp.maximum(dist, 1).astype(F32)
    large = max_exact + (jnp.log(d / max_exact) / math.log(REL_MAX_DIST / max_exact)
                         * (REL_BUCKETS - max_exact)).astype(jnp.int32)
    large = jnp.minimum(large, REL_BUCKETS - 1)
    return jnp.where(dist < max_exact, dist, large)


def _swa_bias_table(rel_bias):
    nq, nk = SWA_BLOCK, 2 * SWA_BLOCK
    dist = nk - 1 - jnp.arange(nq + nk - 1)
    per_dist = rel_bias.astype(F32)[_t5_causal_bucket(jnp.maximum(dist, 0))]
    per_dist = jnp.where(((dist >= 0) & (dist < SWA_BLOCK))[:, None], per_dist, NEG_INF)
    u = jnp.pad(per_dist.T, ((0, 0), (0, 1)))
    skew = jnp.tile(u, (1, nq))[:, :nq * (nq + nk - 1)].reshape(-1, nq, nq + nk - 1)
    return skew[:, :, nq - 1:]


def _swa_kernel(sink_ref, q_ref, kv_ref, pkv_ref, bias_ref, o_ref, *, blocks_per_batch):
    half = SWA_HEAD_DIM
    first = (pl.program_id(0) % blocks_per_batch) == 0
    kv = jnp.concatenate([pkv_ref[...], kv_ref[...]], axis=0)
    k = kv[:, :LANES]
    v = kv[:, LANES:]
    lo = lax.broadcasted_iota(jnp.int32, k.shape, 1) < half
    zero = jnp.zeros_like(k)

    def swap(t):
        return jnp.concatenate([t[:, half:], t[:, :half]], axis=1)

    def padded(t):
        tr = swap(t)
        return ((jnp.where(lo, t, zero), jnp.where(lo, zero, tr)),
                (jnp.where(lo, tr, zero), jnp.where(lo, zero, t)))

    kpad = padded(k)
    vpad = padded(v)
    col = lax.broadcasted_iota(jnp.int32, (SWA_BLOCK, 2 * SWA_BLOCK), 1)
    no_prev = jnp.logical_and(first, col < SWA_BLOCK)
    out_lo = lax.broadcasted_iota(jnp.int32, (SWA_BLOCK, LANES), 1) < half
    group = SWA_HEADS // SWA_KV_HEADS
    scores = [_dot_nt(q_ref[:, (head // 2) * LANES:(head // 2 + 1) * LANES], kpad[head // group][head % 2])
              for head in range(SWA_HEADS)]
    for p in range(SWA_HEADS // 2):
        acc = None
        inv = []
        for par in range(2):
            head = 2 * p + par
            s = scores[head] * (SWA_HEAD_DIM ** -0.5) + bias_ref[head]
            s = jnp.where(no_prev, NEG_INF, s)
            sink = sink_ref[head]
            m = jnp.maximum(jnp.max(s, axis=-1, keepdims=True), sink)
            e = jnp.exp(s - m)
            d = jnp.sum(e, axis=-1, keepdims=True) + jnp.exp(sink - m)
            pv = _dot(e, vpad[head // group][par])
            acc = pv if acc is None else acc + pv
            inv.append(1.0 / d)
        o_ref[:, p * LANES:(p + 1) * LANES] = (acc * jnp.where(out_lo, inv[0], inv[1])).astype(o_ref.dtype)


def _swa_attention(proj, q_col, kv_col, sinks, bias_tab, seq):
    t = proj.shape[0]
    qw = SWA_HEADS * SWA_HEAD_DIM
    kvw = 2 * SWA_KV_HEADS * SWA_HEAD_DIM
    bpb = seq // SWA_BLOCK
    return pl.pallas_call(
        functools.partial(_swa_kernel, blocks_per_batch=bpb),
        grid=(t // SWA_BLOCK,),
        in_specs=[pl.BlockSpec(memory_space=pltpu.SMEM),
                  pl.BlockSpec((SWA_BLOCK, qw), lambda i: (i, q_col)),
                  pl.BlockSpec((SWA_BLOCK, kvw), lambda i: (i, kv_col)),
                  pl.BlockSpec((SWA_BLOCK, kvw), lambda i: (jnp.maximum(i - 1, 0), kv_col)),
                  pl.BlockSpec((SWA_HEADS, SWA_BLOCK, 2 * SWA_BLOCK), lambda i: (0, 0, 0))],
        out_specs=pl.BlockSpec((SWA_BLOCK, qw), lambda i: (i, 0)),
        out_shape=jax.ShapeDtypeStruct((t, qw), BF16),
        compiler_params=_params("parallel"),
        name="swa_attention",
    )(sinks, proj, proj, proj, bias_tab)


def _gdn_pre_kernel(q_ref, k_ref, v_ref, hq_ref, hk_ref, hv_ref, ba_ref, cw_ref, alog_ref, dtb_ref,
                    u_ref, w_ref, qd_ref, ai_ref, kdt_ref, gl_ref, *, pairs_per_batch):
    n = GDN_PAIR
    c = GDN_CHUNK
    first = (pl.program_id(0) % pairs_per_batch) == 0
    row = lax.broadcasted_iota(jnp.int32, (n, n), 0)
    col = lax.broadcasted_iota(jnp.int32, (n, n), 1)
    same = (row >= c) == (col >= c)
    tril = jnp.logical_and(same, row >= col)
    strict = jnp.logical_and(same, row > col)
    eye = jnp.where(row == col, 1.0, 0.0).astype(F32)
    rowc = lax.broadcasted_iota(jnp.int32, (n, 1), 0)

    ba = ba_ref[...]
    beta_all = _sigmoid(ba)
    x = ba + dtb_ref[...]
    softplus = jnp.maximum(x, 0.0) + jnp.log1p(jnp.exp(-jnp.abs(x)))
    g_all = -jnp.exp(alog_ref[...]) * softplus
    gc_all = _dot3(jnp.where(tril, 1.0, 0.0).astype(F32), g_all)
    gct_all = gc_all.T

    def conv_silu(cur_ref, halo_ref, which, sl):
        cur = cur_ref[:, sl].astype(F32)
        halo = jnp.where(first, 0.0, halo_ref[:, sl].astype(F32))
        ext = jnp.concatenate([halo, cur], axis=0)
        base = which * GDN_HEADS * GDN_HEAD_DIM
        cw = cw_ref[:, base + sl.start:base + sl.stop]
        y = cw[GDN_CONV - 1:GDN_CONV] * cur
        for j in range(1, GDN_CONV):
            y = y + cw[GDN_CONV - 1 - j:GDN_CONV - j] * pltpu.roll(ext, j, axis=0)[BF16_ROWS:]
        return _silu(y)

    def l2norm(t):
        return t * lax.rsqrt(jnp.sum(t * t, axis=-1, keepdims=True) + 1e-6)

    heads = range(GDN_HEADS)
    sls = [slice(h * LANES, (h + 1) * LANES) for h in heads]
    dotf = functools.partial(jnp.dot, preferred_element_type=F32)

    a_mats, rhs = [], []
    for h in heads:
        sl = sls[h]
        q = l2norm(conv_silu(q_ref, hq_ref, 0, sl)) * (GDN_HEAD_DIM ** -0.5)
        k = l2norm(conv_silu(k_ref, hk_ref, 1, sl))
        v = conv_silu(v_ref, hv_ref, 2, sl)
        beta = beta_all[:, h:h + 1]
        gc = gc_all[:, GDN_HEADS + h:GDN_HEADS + h + 1]
        gcr = gct_all[GDN_HEADS + h:GDN_HEADS + h + 1, :]
        kb = k * beta
        kq = _dot_nt(jnp.concatenate([kb, q], axis=0), k)
        decay = jnp.where(tril, jnp.exp(jnp.where(tril, gc - gcr, 0.0)), 0.0)
        a_mats.append(jnp.where(strict, kq[:n] * decay, 0.0))
        egc = jnp.exp(gc)
        rhs.append(jnp.concatenate([v * beta, kb * egc], axis=1))
        gl_e = gc[c - 1:c, :]
        gl_o = gc[n - 1:n, :]
        kd = k * jnp.exp(jnp.where(rowc < c, gl_e, gl_o) - gc)
        qd_ref[:, sl] = (q * egc).astype(qd_ref.dtype)
        ai_ref[:, sl] = (kq[n:] * decay).astype(ai_ref.dtype)
        kdt_ref[:, sl] = kd.T.astype(kdt_ref.dtype)
        gl_ref[0, 0:1, sl] = jnp.broadcast_to(jnp.exp(gl_e), (1, LANES))
        gl_ref[0, 1:2, sl] = jnp.broadcast_to(jnp.exp(gl_o), (1, LANES))

    def dot3(a_split, b_split):
        (ah, al), (bh, bl) = a_split, b_split
        return dotf(jnp.concatenate([ah, al, ah], axis=1), jnp.concatenate([bh, bh, bl], axis=0))

    blk = row ^ col
    lower = row > col
    base_blk = 8
    diag = jnp.logical_and(lower, blk < base_blk)
    zero_bf = jnp.zeros((n, n), BF16)
    a_splits = [_split(a) for a in a_mats]
    d_splits = [tuple(jnp.where(diag, t, zero_bf) for t in s) for s in a_splits]
    d2s = [dot3(d, d) for d in d_splits]
    xs = []
    for h in heads:
        p = eye - jnp.where(diag, a_mats[h], 0.0)
        d2 = _split(d2s[h])
        ps_ = _split(p)
        m = dot3(tuple(jnp.concatenate([s, t], axis=0) for s, t in zip(d2, ps_)), d2)
        d2s[h] = m[:n]
        xs.append(p + m[n:])
    for h in heads:
        xs[h] = xs[h] + dot3(_split(xs[h]), _split(d2s[h]))
    b = base_blk
    while b < c:
        off = jnp.logical_and(lower, jnp.logical_and(blk >= b, blk < 2 * b))
        o_splits = [tuple(jnp.where(off, t, zero_bf) for t in s) for s in a_splits]
        x_splits = [_split(x) for x in xs]
        ys = [dot3(o_splits[h], x_splits[h]) for h in heads]
        for h in heads:
            xs[h] = xs[h] - dot3(x_splits[h], _split(ys[h]))
        b *= 2
    for h in heads:
        sol = dot3(_split(xs[h]), _split(rhs[h]))
        u_ref[:, sls[h]] = sol[:, :LANES]
        w_ref[:, sls[h]] = sol[:, LANES:].astype(w_ref.dtype)


def _gdn_prepass(proj, ba, conv_w, alog_row, dtb_row, seq):
    t = proj.shape[0]
    gw = GDN_HEADS * GDN_HEAD_DIM
    n = GDN_PAIR
    ppb = seq // n
    hb = n // BF16_ROWS

    def cur(cb):
        return pl.BlockSpec((n, gw), lambda i: (i, cb))

    def halo(cb):
        return pl.BlockSpec((BF16_ROWS, gw), lambda i: (jnp.maximum(i * hb - 1, 0), cb))

    row_spec = pl.BlockSpec((n, gw), lambda i: (i, 0))
    return pl.pallas_call(
        functools.partial(_gdn_pre_kernel, pairs_per_batch=ppb),
        grid=(t // n,),
        in_specs=[cur(0), cur(1), cur(2), halo(0), halo(1), halo(2),
                  pl.BlockSpec((n, LANES), lambda i: (i, 0)),
                  pl.BlockSpec((GDN_CONV, 3 * gw), lambda i: (0, 0)),
                  pl.BlockSpec((1, LANES), lambda i: (0, 0)),
                  pl.BlockSpec((1, LANES), lambda i: (0, 0))],
        out_specs=[row_spec, row_spec, row_spec, row_spec, row_spec,
                   pl.BlockSpec((1, 2, gw), lambda i: (i, 0, 0))],
        out_shape=[jax.ShapeDtypeStruct((t, gw), F32),
                   jax.ShapeDtypeStruct((t, gw), BF16),
                   jax.ShapeDtypeStruct((t, gw), BF16),
                   jax.ShapeDtypeStruct((t, gw), BF16),
                   jax.ShapeDtypeStruct((t, gw), BF16),
                   jax.ShapeDtypeStruct((t // n, 2, gw), F32)],
        compiler_params=_params("parallel"),
        name="gdn_prepass",
    )(proj, proj, proj, proj, proj, proj, ba, conv_w, alog_row, dtb_row)


def _gdn_scan_kernel(u_ref, w_ref, qd_ref, ai_ref, kdt_ref, gl_ref, z_ref, nw_ref, o_ref, st_ref):
    c = GDN_CHUNK

    @pl.when(pl.program_id(1) == 0)
    def _():
        st_ref[...] = jnp.zeros_like(st_ref)

    zeros = jnp.zeros((c, LANES), F32)
    nw = nw_ref[...]
    heads = range(GDN_HEADS)
    sls = [slice(h * LANES, (h + 1) * LANES) for h in heads]
    states = [st_ref[h] for h in heads]
    for ci in range(2):
        rs = slice(ci * c, (ci + 1) * c)
        sbs = [s.astype(BF16) for s in states]
        ws = [_dot(w_ref[rs, sls[h]], sbs[h]) for h in heads]
        qs = [_dot(qd_ref[rs, sls[h]], sbs[h]) for h in heads]
        vpads = []
        for h in heads:
            vnew = u_ref[rs, sls[h]] - ws[h]
            vpads.append(jnp.concatenate([vnew, zeros] if ci == 0 else [zeros, vnew], axis=0).astype(BF16))
        outs = [qs[h] + _dot(ai_ref[rs, sls[h]], vpads[h]) for h in heads]
        states = [states[h] * gl_ref[0, ci:ci + 1, sls[h]] + _dot(kdt_ref[:, sls[h]], vpads[h]) for h in heads]
        for h in heads:
            o = outs[h]
            z = z_ref[rs, sls[h]].astype(F32)
            o = o * lax.rsqrt(jnp.mean(o * o, axis=-1, keepdims=True) + 1e-6) * nw * _silu(z)
            o_ref[rs, sls[h]] = o.astype(o_ref.dtype)
    for h in heads:
        st_ref[h] = states[h]


def _gdn_scan(u, w, qd, ai, kdt, gl, proj, z_col, norm_w_row, batch, seq):
    t = u.shape[0]
    gw = GDN_HEADS * GDN_HEAD_DIM
    n = GDN_PAIR
    ppb = seq // n
    row_spec = pl.BlockSpec((n, gw), lambda b, i: (b * ppb + i, 0))
    return pl.pallas_call(
        _gdn_scan_kernel,
        grid=(batch, ppb),
        in_specs=[row_spec, row_spec, row_spec, row_spec, row_spec,
                  pl.BlockSpec((1, 2, gw), lambda b, i: (b * ppb + i, 0, 0)),
                  pl.BlockSpec((n, gw), lambda b, i: (b * ppb + i, z_col)),
                  pl.BlockSpec((1, LANES), lambda b, i: (0, 0))],
        out_specs=row_spec,
        out_shape=jax.ShapeDtypeStruct((t, gw), BF16),
        scratch_shapes=[pltpu.VMEM((GDN_HEADS, GDN_HEAD_DIM, GDN_HEAD_DIM), F32)],
        compiler_params=_params("arbitrary", "arbitrary"),
        name="gdn_scan",
    )(u, w, qd, ai, kdt, gl, proj, norm_w_row)


def _merge_kernel(x_ref, a_ref, g_ref, wgs_ref, wgg_ref, wbs_ref, wbg_ref, o_ref):
    x = x_ref[...]
    gate_s = _sigmoid(jnp.dot(x, wgs_ref[...], preferred_element_type=F32))
    gate_g = _sigmoid(jnp.dot(x, wgg_ref[...], preferred_element_type=F32))
    y_s = jnp.dot(a_ref[...], wbs_ref[...], preferred_element_type=F32)
    y_g = jnp.dot(g_ref[...], wbg_ref[...], preferred_element_type=F32)
    o_ref[...] = (gate_s * y_s + gate_g * y_g).astype(o_ref.dtype)


def _merge(xb, attn, gdn, wgs, wgg, wbs, wbg, tm, tn):
    t, d = xb.shape
    ws = attn.shape[1]
    wg = gdn.shape[1]
    return pl.pallas_call(
        _merge_kernel,
        grid=(t // tm, d // tn),
        in_specs=[pl.BlockSpec((tm, d), lambda i, j: (i, 0)),
                  pl.BlockSpec((tm, ws), lambda i, j: (i, 0)),
                  pl.BlockSpec((tm, wg), lambda i, j: (i, 0)),
                  pl.BlockSpec((d, tn), lambda i, j: (0, j)),
                  pl.BlockSpec((d, tn), lambda i, j: (0, j)),
                  pl.BlockSpec((ws, tn), lambda i, j: (0, j)),
                  pl.BlockSpec((wg, tn), lambda i, j: (0, j))],
        out_specs=pl.BlockSpec((tm, tn), lambda i, j: (i, j)),
        out_shape=jax.ShapeDtypeStruct((t, d), BF16),
        compiler_params=_params("parallel", "arbitrary"),
        name="branch_merge",
    )(xb, attn, gdn, wgs, wgg, wbs, wbg)


def _mixo_ln_kernel(m_ref, x_ref, w_ref, g_ref, b_ref, o_ref, *, alpha):
    y = alpha * x_ref[...] + jnp.dot(m_ref[...], w_ref[...], preferred_element_type=F32)
    o_ref[...] = _layer_norm(y, g_ref[...], b_ref[...])


def _mixo_ln(mixed, x, w, g, b, alpha, tm):
    t, d = x.shape
    vec = pl.BlockSpec((1, d), lambda i: (0, 0))
    return pl.pallas_call(
        functools.partial(_mixo_ln_kernel, alpha=alpha),
        grid=(t // tm,),
        in_specs=[pl.BlockSpec((tm, d), lambda i: (i, 0)),
                  pl.BlockSpec((tm, d), lambda i: (i, 0)),
                  pl.BlockSpec((d, d), lambda i: (0, 0)),
                  vec, vec],
        out_specs=pl.BlockSpec((tm, d), lambda i: (i, 0)),
        out_shape=jax.ShapeDtypeStruct((t, d), F32),
        compiler_params=_params("parallel"),
        name="mixo_layernorm",
    )(mixed, x, w, g, b)


def _mem_kernel(x_ref, wq_ref, kv_ref, wo_ref, g_ref, b_ref, o_ref, *, alpha):
    x = x_ref[...]
    q = jnp.dot(x.astype(BF16), wq_ref[...], preferred_element_type=F32)
    mw = MEM_HEADS * MEM_HEAD_DIM
    outs = []
    qb = q.astype(BF16)
    scores = [_dot_nt(qb[:, h * MEM_HEAD_DIM:(h + 1) * MEM_HEAD_DIM],
                      kv_ref[:, h * MEM_HEAD_DIM:(h + 1) * MEM_HEAD_DIM]) for h in range(MEM_HEADS)]
    for h in range(MEM_HEADS):
        s = scores[h] * (MEM_HEAD_DIM ** -0.5)
        m = jnp.max(s, axis=-1, keepdims=True)
        e = jnp.exp(s - m)
        d = jnp.sum(e, axis=-1, keepdims=True)
        vh = kv_ref[:, mw + h * MEM_HEAD_DIM:mw + (h + 1) * MEM_HEAD_DIM]
        outs.append(_dot(e, vh) / d)
    o = jnp.concatenate(outs, axis=1)
    y = alpha * x + _dot(o, wo_ref[...])
    o_ref[...] = _layer_norm(y, g_ref[...], b_ref[...])


def _mem_attention_ln(x, wq, kv, wo, g, b, alpha, batch, seq, mem_len, tm):
    t, d = x.shape
    mw = MEM_HEADS * MEM_HEAD_DIM
    tpb = seq // tm
    vec = pl.BlockSpec((1, d), lambda bi, i: (0, 0))
    return pl.pallas_call(
        functools.partial(_mem_kernel, alpha=alpha),
        grid=(batch, tpb),
        in_specs=[pl.BlockSpec((tm, d), lambda bi, i: (bi * tpb + i, 0)),
                  pl.BlockSpec((d, mw), lambda bi, i: (0, 0)),
                  pl.BlockSpec((mem_len, 2 * mw), lambda bi, i: (bi, 0)),
                  pl.BlockSpec((mw, d), lambda bi, i: (0, 0)),
                  vec, vec],
        out_specs=pl.BlockSpec((tm, d), lambda bi, i: (bi * tpb + i, 0)),
        out_shape=jax.ShapeDtypeStruct((t, d), F32),
        compiler_params=_params("parallel", "arbitrary"),
        name="mem_attention_layernorm",
    )(x, wq, kv, wo, g, b)


def _ffn_kernel(x_ref, halo_ref, wg_ref, wu_ref, cg_ref, cu_ref, bg_ref, bu_ref, wd_ref, g_ref, b_ref,
                o_ref, xb_ref, h0_ref, h1_ref, act_ref, *, alpha, tiles_per_batch, n_up, tn, sub, tdn, rows):
    j = pl.program_id(1)
    hr = BF16_ROWS
    h_refs = (h0_ref, h1_ref)

    def up_dots(h_ref):
        xb = xb_ref[...]
        for n in range(tn // sub):
            cs = slice(n * sub, (n + 1) * sub)
            h_ref[:, cs] = jnp.dot(xb, wg_ref[:, cs], preferred_element_type=F32)
            h_ref[:, tn + n * sub:tn + (n + 1) * sub] = jnp.dot(xb, wu_ref[:, cs],
                                                               preferred_element_type=F32)

    def activate(h_ref, slab):
        base = pl.multiple_of(slab * tn, tn)
        tm = act_ref.shape[0]
        for n in range(tn // sub):
            cs = slice(n * sub, (n + 1) * sub)
            us = slice(tn + n * sub, tn + (n + 1) * sub)
            cwg, cwu = cg_ref[:, cs], cu_ref[:, cs]
            bgv, buv = bg_ref[:, cs], bu_ref[:, cs]
            for r0 in range(0, tm, rows):
                def conv(cols, cw, bias):
                    y = bias
                    for k in range(FFN_CONV):
                        tap = h_ref[hr + r0 - k:hr + r0 - k + rows, cols]
                        y = y + cw[FFN_CONV - 1 - k:FFN_CONV - k] * tap
                    return y

                act = _silu(conv(cs, cwg, bgv)) * conv(us, cwu, buv)
                act_ref[r0:r0 + rows, pl.ds(base + n * sub, sub)] = act.astype(BF16)

    @pl.when(j == 0)
    def _():
        first = (pl.program_id(0) % tiles_per_batch) == 0
        xb_ref[:hr, :] = jnp.where(first, 0.0, halo_ref[...]).astype(BF16)
        xb_ref[hr:, :] = x_ref[...].astype(BF16)
        up_dots(h_refs[0])

    steady = jnp.logical_and(j > 0, j < n_up)
    for parity in range(2):
        @pl.when(jnp.logical_and(steady, j % 2 == parity))
        def _():
            activate(h_refs[1 - parity], j - 1)
            up_dots(h_refs[parity])

    @pl.when(j == n_up)
    def _():
        activate(h_refs[(n_up - 1) % 2], n_up - 1)

    @pl.when(j > n_up)
    def _():
        col = pl.multiple_of((j - n_up - 1) * tdn, tdn)
        o_ref[:, pl.ds(col, tdn)] = jnp.dot(act_ref[...], wd_ref[...], preferred_element_type=F32)

    @pl.when(j == pl.num_programs(1) - 1)
    def _():
        y = alpha * x_ref[...] + o_ref[...]
        o_ref[...] = _layer_norm(y, g_ref[...], b_ref[...])


def _ffn_ln(x, wg, wu, cg, cu, bg, bu, wd, g, b, alpha, seq, tm, tn, tdn, sub, rows):
    t, d = x.shape
    ff = wg.shape[1]
    tpb = seq // tm
    hb = tm // BF16_ROWS
    n_up = ff // tn
    n_dn = d // tdn
    vec = pl.BlockSpec((1, d), lambda i, j: (0, 0))

    def up_col(i, j):
        return (0, jnp.minimum(j, n_up - 1))

    def act_col(i, j):
        return (0, jnp.clip(j - 1, 0, n_up - 1))

    return pl.pallas_call(
        functools.partial(_ffn_kernel, alpha=alpha, tiles_per_batch=tpb, n_up=n_up, tn=tn, sub=sub,
                          tdn=tdn, rows=rows),
        grid=(t // tm, n_up + 1 + n_dn),
        in_specs=[pl.BlockSpec((tm, d), lambda i, j: (i, 0)),
                  pl.BlockSpec((BF16_ROWS, d), lambda i, j: (jnp.maximum(i * hb - 1, 0), 0)),
                  pl.BlockSpec((d, tn), up_col),
                  pl.BlockSpec((d, tn), up_col),
                  pl.BlockSpec((FFN_CONV, tn), act_col),
                  pl.BlockSpec((FFN_CONV, tn), act_col),
                  pl.BlockSpec((1, tn), act_col),
                  pl.BlockSpec((1, tn), act_col),
                  pl.BlockSpec((ff, tdn), lambda i, j: (0, jnp.maximum(j - n_up - 1, 0))),
                  vec, vec],
        out_specs=pl.BlockSpec((tm, d), lambda i, j: (i, 0)),
        out_shape=jax.ShapeDtypeStruct((t, d), F32),
        scratch_shapes=[pltpu.VMEM((BF16_ROWS + tm, d), BF16),
                        pltpu.VMEM((BF16_ROWS + tm, 2 * tn), F32),
                        pltpu.VMEM((BF16_ROWS + tm, 2 * tn), F32),
                        pltpu.VMEM((tm, ff), BF16)],
        compiler_params=_params("parallel", "arbitrary"),
        name="ffn_layernorm",
    )(x, x, wg, wu, cg, cu, bg, bu, wd, g, b)


def _pad_cols(a, n):
    return jnp.pad(a, ((0, 0), (0, n - a.shape[1])))


def _layer(x2d, mem2d, batch, seq, mem_len, p, bias_tab, alpha, tiles):
    d = x2d.shape[1]
    sq = SWA_HEADS * SWA_HEAD_DIM
    skv = SWA_KV_HEADS * SWA_HEAD_DIM
    gw = GDN_HEADS * GDN_HEAD_DIM
    w_in = p["w_in"]
    o_gdn = sq + 2 * skv
    o_ba = o_gdn + 4 * gw
    o_gate = o_ba + 2 * GDN_HEADS
    w_proj = jnp.concatenate([w_in[:, o_gdn:o_ba], w_in[:, :o_gdn]], axis=1).astype(BF16)
    w_ba = _pad_cols(w_in[:, o_ba:o_gate], LANES).astype(BF16)
    w_gs = w_in[:, o_gate:o_gate + d].astype(BF16)
    w_gg = w_in[:, o_gate + d:o_gate + 2 * d].astype(BF16)

    proj, xb = _proj(x2d, w_proj, tiles["proj_tm"], tiles["proj_tn"])
    ba = _matmul(xb, w_ba, tiles["proj_tm"], LANES, F32)

    attn = _swa_attention(proj, (4 * gw) // sq, (4 * gw + sq) // (2 * skv), p["swa_sinks"], bias_tab, seq)

    zpad = jnp.zeros((GDN_HEADS,), F32)
    alog_row = _pad_cols(jnp.concatenate([zpad, p["gdn_a_log"]])[None, :], LANES)
    dtb_row = _pad_cols(jnp.concatenate([zpad, p["gdn_dt_bias"]])[None, :], LANES)
    u, w, qd, ai, kdt, gl = _gdn_prepass(proj, ba, p["gdn_conv_w"], alog_row, dtb_row, seq)
    gdn = _gdn_scan(u, w, qd, ai, kdt, gl, proj, 3, p["gdn_norm_w"][None, :], batch, seq)

    mixed = _merge(xb, attn, gdn, w_gs, w_gg, p["w_br_swa"].astype(BF16), p["w_br_gdn"].astype(BF16),
                   tiles["merge_tm"], tiles["merge_tn"])
    x1 = _mixo_ln(mixed, x2d, p["w_mix_o"].astype(BF16), p["ln1_g"][None, :], p["ln1_b"][None, :],
                  alpha, tiles["ln_tm"])

    kv = _matmul(mem2d.astype(BF16), p["w_mem_kv"].astype(BF16), mem_len, tiles["kv_tn"], BF16)
    x2 = _mem_attention_ln(x1, p["w_mem_q"].astype(BF16), kv, p["w_mem_o"].astype(BF16),
                           p["ln2_g"][None, :], p["ln2_b"][None, :], alpha, batch, seq, mem_len,
                           tiles["mem_tm"])

    dff = p["w_down"].shape[0]
    tn = tiles["ffn_tn"]
    ffp = -(-dff // tn) * tn
    w_up = p["w_up"]
    cw = p["ffn_conv_w"]
    cb = p["ffn_conv_b"][None, :]
    x3 = _ffn_ln(x2,
                 _pad_cols(w_up[:, :dff], ffp).astype(BF16), _pad_cols(w_up[:, dff:], ffp).astype(BF16),
                 _pad_cols(cw[:, :dff], ffp), _pad_cols(cw[:, dff:], ffp),
                 _pad_cols(cb[:, :dff], ffp), _pad_cols(cb[:, dff:], ffp),
                 jnp.pad(p["w_down"], ((0, ffp - dff), (0, 0))).astype(BF16),
                 p["ln3_g"][None, :], p["ln3_b"][None, :], alpha, seq, tiles["ffn_tm"], tn,
                 tiles["ffn_tdn"], tiles["ffn_sub"], tiles["ffn_rows"])
    return x3


_TILES = dict(proj_tm=1024, proj_tn=768, merge_tm=512, merge_tn=512, ln_tm=512, kv_tn=512,
              mem_tm=512, ffn_tm=512, ffn_tn=512, ffn_tdn=512, ffn_sub=256, ffn_rows=32)

_PER_LAYER = ("w_in", "swa_sinks", "gdn_conv_w", "gdn_a_log", "gdn_dt_bias", "gdn_norm_w", "w_br_swa",
              "w_br_gdn", "w_mix_o", "ln1_g", "ln1_b", "w_mem_q", "w_mem_kv", "w_mem_o", "ln2_g", "ln2_b",
              "w_up", "ffn_conv_w", "ffn_conv_b", "w_down", "ln3_g", "ln3_b")


def _forward(x, mem, rel_bias, weights, tiles):
    batch, seq, d = x.shape
    mem_len = mem.shape[1]
    depth = weights["w_in"].shape[0]
    alpha = (2 * depth) ** 0.25
    bias_tab = _swa_bias_table(rel_bias)
    x2d = x.reshape(batch * seq, d)
    mem2d = mem.reshape(batch * mem_len, d)
    for l in range(depth):
        p = {name: weights[name][l] for name in _PER_LAYER}
        x2d = _layer(x2d, mem2d, batch, seq, mem_len, p, bias_tab, alpha, tiles)
    return x2d.reshape(batch, seq, d)


def kernel(x, mem, w_in, rel_bias, swa_sinks, gdn_conv_w, gdn_a_log, gdn_dt_bias, gdn_norm_w, w_br_swa, w_br_gdn, w_mix_o, ln1_g, ln1_b, w_mem_q, w_mem_kv, w_mem_o, ln2_g, ln2_b, w_up, ffn_conv_w, ffn_conv_b, w_down, ln3_g, ln3_b):
    weights = dict(w_in=w_in, swa_sinks=swa_sinks, gdn_conv_w=gdn_conv_w, gdn_a_log=gdn_a_log,
                   gdn_dt_bias=gdn_dt_bias, gdn_norm_w=gdn_norm_w, w_br_swa=w_br_swa, w_br_gdn=w_br_gdn,
                   w_mix_o=w_mix_o, ln1_g=ln1_g, ln1_b=ln1_b, w_mem_q=w_mem_q, w_mem_kv=w_mem_kv,
                   w_mem_o=w_mem_o, ln2_g=ln2_g, ln2_b=ln2_b, w_up=w_up, ffn_conv_w=ffn_conv_w,
                   ffn_conv_b=ffn_conv_b, w_down=w_down, ln3_g=ln3_g, ln3_b=ln3_b)
    return _forward(x, mem, rel_bias, weights, _TILES)
```

```python
import functools
import math

import jax
import jax.numpy as jnp
from jax import lax
from jax.experimental import pallas as pl
from jax.experimental.pallas import tpu as pltpu

F32 = jnp.float32
BF16 = jnp.bfloat16

LANES = 128
BF16_ROWS = 16
VMEM_LIMIT = 56 * 1024 * 1024

SWA_HEADS = 16
SWA_KV_HEADS = 2
SWA_HEAD_DIM = 64
SWA_BLOCK = 128
REL_BUCKETS = 32
REL_MAX_DIST = 128
GDN_HEADS = 8
GDN_HEAD_DIM = 128
GDN_CONV = 4
GDN_CHUNK = 64
GDN_PAIR = 2 * GDN_CHUNK
MEM_HEADS = 4
MEM_HEAD_DIM = 128
FFN_CONV = 3
NORM_EPS = 1e-5
NEG_INF = -1e30


def _params(*sem):
    return pltpu.CompilerParams(dimension_semantics=sem, vmem_limit_bytes=VMEM_LIMIT)


def _dot(a, b):
    return jnp.dot(a.astype(BF16), b.astype(BF16), preferred_element_type=F32)


def _dot_nt(a, b):
    return lax.dot_general(a.astype(BF16), b.astype(BF16), (((1,), (1,)), ((), ())),
                           preferred_element_type=F32)


def _split(a):
    hi = a.astype(BF16)
    lo = (a - hi.astype(F32)).astype(BF16)
    return hi, lo


def _dot3(a, b):
    ah, al = _split(a)
    bh, bl = _split(b)
    d = functools.partial(jnp.dot, preferred_element_type=F32)
    return d(ah, bh) + (d(al, bh) + d(ah, bl))


def _sigmoid(x):
    return 1.0 / (1.0 + jnp.exp(-x))


def _silu(x):
    return x * _sigmoid(x)


def _layer_norm(y, g, b):
    mu = jnp.mean(y, axis=-1, keepdims=True)
    yc = y - mu
    var = jnp.mean(yc * yc, axis=-1, keepdims=True)
    return yc * lax.rsqrt(var + NORM_EPS) * g + b


def _mm_kernel(x_ref, w_ref, o_ref):
    o_ref[...] = jnp.dot(x_ref[...], w_ref[...], preferred_element_type=F32).astype(o_ref.dtype)


def _matmul(x, w, tm, tn, out_dtype):
    m, k = x.shape
    n = w.shape[1]
    assert m % tm == 0 and n % tn == 0
    return pl.pallas_call(
        _mm_kernel,
        grid=(m // tm, n // tn),
        in_specs=[pl.BlockSpec((tm, k), lambda i, j: (i, 0)),
                  pl.BlockSpec((k, tn), lambda i, j: (0, j))],
        out_specs=pl.BlockSpec((tm, tn), lambda i, j: (i, j)),
        out_shape=jax.ShapeDtypeStruct((m, n), out_dtype),
        compiler_params=_params("parallel", "arbitrary"),
        name="matmul",
    )(x, w)


def _proj_kernel(x_ref, w_ref, o_ref, xb_ref):
    @pl.when(pl.program_id(1) == 0)
    def _():
        xb_ref[...] = x_ref[...].astype(xb_ref.dtype)

    o_ref[...] = jnp.dot(xb_ref[...], w_ref[...], preferred_element_type=F32).astype(o_ref.dtype)


def _tile_major(w, tn):
    k, n = w.shape
    assert n % tn == 0
    return w.reshape(k, n // tn, tn).transpose(1, 0, 2)


def _proj(x, w, tm):
    m, k = x.shape
    nt, _, tn = w.shape
    n = nt * tn
    assert m % tm == 0
    return pl.pallas_call(
        _proj_kernel,
        grid=(m // tm, nt),
        in_specs=[pl.BlockSpec((tm, k), lambda i, j: (i, 0)),
                  pl.BlockSpec((None, k, tn), lambda i, j: (j, 0, 0))],
        out_specs=[pl.BlockSpec((tm, tn), lambda i, j: (i, j)),
                   pl.BlockSpec((tm, k), lambda i, j: (i, 0))],
        out_shape=[jax.ShapeDtypeStruct((m, n), BF16), jax.ShapeDtypeStruct((m, k), BF16)],
        compiler_params=_params("parallel", "arbitrary"),
        name="input_proj",
    )(x, w)


def _t5_causal_bucket(dist):
    max_exact = REL_BUCKETS // 2
    d = jnp.maximum(dist, 1).astype(F32)
    large = max_exact + (jnp.log(d / max_exact) / math.log(REL_MAX_DIST / max_exact)
                         * (REL_BUCKETS - max_exact)).astype(jnp.int32)
    large = jnp.minimum(large, REL_BUCKETS - 1)
    return jnp.where(dist < max_exact, dist, large)


def _swa_bias_table(rel_bias):
    nq, nk = SWA_BLOCK, 2 * SWA_BLOCK
    dist = nk - 1 - jnp.arange(nq + nk - 1)
    per_dist = rel_bias.astype(F32)[_t5_causal_bucket(jnp.maximum(dist, 0))]
    per_dist = jnp.where(((dist >= 0) & (dist < SWA_BLOCK))[:, None], per_dist, NEG_INF)
    u = jnp.pad(per_dist.T, ((0, 0), (0, 1)))
    skew = jnp.tile(u, (1, nq))[:, :nq * (nq + nk - 1)].reshape(-1, nq, nq + nk - 1)
    return skew[:, :, nq - 1:]


def _swa_kernel(sink_ref, q_ref, kv_ref, pkv_ref, bias_ref, o_ref, *, blocks_per_batch):
    half = SWA_HEAD_DIM
    first = (pl.program_id(0) % blocks_per_batch) == 0
    kv = jnp.concatenate([pkv_ref[...], kv_ref[...]], axis=0)
    k = kv[:, :LANES]
    v = kv[:, LANES:]
    lo = lax.broadcasted_iota(jnp.int32, k.shape, 1) < half
    zero = jnp.zeros_like(k)

    def swap(t):
        return jnp.concatenate([t[:, half:], t[:, :half]], axis=1)

    def padded(t):
        tr = swap(t)
        return ((jnp.where(lo, t, zero), jnp.where(lo, zero, tr)),
                (jnp.where(lo, tr, zero), jnp.where(lo, zero, t)))

    kpad = padded(k)
    vpad = padded(v)
    col = lax.broadcasted_iota(jnp.int32, (SWA_BLOCK, 2 * SWA_BLOCK), 1)
    no_prev = jnp.logical_and(first, col < SWA_BLOCK)
    out_lo = lax.broadcasted_iota(jnp.int32, (SWA_BLOCK, LANES), 1) < half
    group = SWA_HEADS // SWA_KV_HEADS
    scores = [_dot_nt(q_ref[:, (head // 2) * LANES:(head // 2 + 1) * LANES], kpad[head // group][head % 2])
              for head in range(SWA_HEADS)]
    for p in range(SWA_HEADS // 2):
        acc = None
        inv = []
        for par in range(2):
            head = 2 * p + par
            s = scores[head] * (SWA_HEAD_DIM ** -0.5) + bias_ref[head]
            s = jnp.where(no_prev, NEG_INF, s)
            sink = sink_ref[head]
            m = jnp.maximum(jnp.max(s, axis=-1, keepdims=True), sink)
            e = jnp.exp(s - m)
            d = jnp.sum(e, axis=-1, keepdims=True) + jnp.exp(sink - m)
            pv = _dot(e, vpad[head // group][par])
            acc = pv if acc is None else acc + pv
            inv.append(1.0 / d)
        o_ref[:, p * LANES:(p + 1) * LANES] = (acc * jnp.where(out_lo, inv[0], inv[1])).astype(o_ref.dtype)


def _swa_attention(proj, q_col, kv_col, sinks, bias_tab, seq):
    t = proj.shape[0]
    qw = SWA_HEADS * SWA_HEAD_DIM
    kvw = 2 * SWA_KV_HEADS * SWA_HEAD_DIM
    bpb = seq // SWA_BLOCK
    return pl.pallas_call(
        functools.partial(_swa_kernel, blocks_per_batch=bpb),
        grid=(t // SWA_BLOCK,),
        in_specs=[pl.BlockSpec(memory_space=pltpu.SMEM),
                  pl.BlockSpec((SWA_BLOCK, qw), lambda i: (i, q_col)),
                  pl.BlockSpec((SWA_BLOCK, kvw), lambda i: (i, kv_col)),
                  pl.BlockSpec((SWA_BLOCK, kvw), lambda i: (jnp.maximum(i - 1, 0), kv_col)),
                  pl.BlockSpec((SWA_HEADS, SWA_BLOCK, 2 * SWA_BLOCK), lambda i: (0, 0, 0))],
        out_specs=pl.BlockSpec((SWA_BLOCK, qw), lambda i: (i, 0)),
        out_shape=jax.ShapeDtypeStruct((t, qw), BF16),
        compiler_params=_params("parallel"),
        name="swa_attention",
    )(sinks, proj, proj, proj, bias_tab)


def _gdn_pre_kernel(q_ref, k_ref, v_ref, hq_ref, hk_ref, hv_ref, ba_ref, cw_ref, alog_ref, dtb_ref,
                    u_ref, w_ref, qd_ref, ai_ref, kdt_ref, gl_ref, *, pairs_per_batch):
    n = GDN_PAIR
    c = GDN_CHUNK
    first = (pl.program_id(0) % pairs_per_batch) == 0
    row = lax.broadcasted_iota(jnp.int32, (n, n), 0)
    col = lax.broadcasted_iota(jnp.int32, (n, n), 1)
    same = (row >= c) == (col >= c)
    tril = jnp.logical_and(same, row >= col)
    strict = jnp.logical_and(same, row > col)
    eye = jnp.where(row == col, 1.0, 0.0).astype(F32)
    rowc = lax.broadcasted_iota(jnp.int32, (n, 1), 0)

    ba = ba_ref[...]
    beta_all = _sigmoid(ba)
    x = ba + dtb_ref[...]
    softplus = jnp.maximum(x, 0.0) + jnp.log1p(jnp.exp(-jnp.abs(x)))
    g_all = -jnp.exp(alog_ref[...]) * softplus
    gc_all = _dot3(jnp.where(tril, 1.0, 0.0).astype(F32), g_all)
    gct_all = gc_all.T

    def conv_silu(cur_ref, halo_ref, which, sl):
        cur = cur_ref[:, sl].astype(F32)
        halo = jnp.where(first, 0.0, halo_ref[:, sl].astype(F32))
        ext = jnp.concatenate([halo, cur], axis=0)
        base = which * GDN_HEADS * GDN_HEAD_DIM
        cw = cw_ref[:, base + sl.start:base + sl.stop]
        y = cw[GDN_CONV - 1:GDN_CONV] * cur
        for j in range(1, GDN_CONV):
            y = y + cw[GDN_CONV - 1 - j:GDN_CONV - j] * pltpu.roll(ext, j, axis=0)[BF16_ROWS:]
        return _silu(y)

    def l2norm(t):
        return t * lax.rsqrt(jnp.sum(t * t, axis=-1, keepdims=True) + 1e-6)

    heads = range(GDN_HEADS)
    sls = [slice(h * LANES, (h + 1) * LANES) for h in heads]
    dotf = functools.partial(jnp.dot, preferred_element_type=F32)

    a_mats, rhs = [], []
    for h in heads:
        sl = sls[h]
        q = l2norm(conv_silu(q_ref, hq_ref, 0, sl)) * (GDN_HEAD_DIM ** -0.5)
        k = l2norm(conv_silu(k_ref, hk_ref, 1, sl))
        v = conv_silu(v_ref, hv_ref, 2, sl)
        beta = beta_all[:, h:h + 1]
        gc = gc_all[:, GDN_HEADS + h:GDN_HEADS + h + 1]
        gcr = gct_all[GDN_HEADS + h:GDN_HEADS + h + 1, :]
        kb = k * beta
        kq = _dot_nt(jnp.concatenate([kb, q], axis=0), k)
        decay = jnp.where(tril, jnp.exp(jnp.where(tril, gc - gcr, 0.0)), 0.0)
        a_mats.append(jnp.where(strict, kq[:n] * decay, 0.0))
        egc = jnp.exp(gc)
        rhs.append(jnp.concatenate([v * beta, kb * egc], axis=1))
        gl_e = gc[c - 1:c, :]
        gl_o = gc[n - 1:n, :]
        kd = k * jnp.exp(jnp.where(rowc < c, gl_e, gl_o) - gc)
        qd_ref[:, sl] = (q * egc).astype(qd_ref.dtype)
        ai_ref[:, sl] = (kq[n:] * decay).astype(ai_ref.dtype)
        kdt_ref[:, sl] = kd.T.astype(kdt_ref.dtype)
        gl_ref[0, 0:1, sl] = jnp.broadcast_to(jnp.exp(gl_e), (1, LANES))
        gl_ref[0, 1:2, sl] = jnp.broadcast_to(jnp.exp(gl_o), (1, LANES))

    def dot3(a_split, b_split):
        (ah, al), (bh, bl) = a_split, b_split
        return dotf(jnp.concatenate([ah, al, ah], axis=1), jnp.concatenate([bh, bh, bl], axis=0))

    blk = row ^ col
    lower = row > col
    base_blk = 8
    diag = jnp.logical_and(lower, blk < base_blk)
    zero_bf = jnp.zeros((n, n), BF16)
    a_splits = [_split(a) for a in a_mats]
    d_splits = [tuple(jnp.where(diag, t, zero_bf) for t in s) for s in a_splits]
    d2s = [dot3(d, d) for d in d_splits]
    xs = []
    for h in heads:
        p = eye - jnp.where(diag, a_mats[h], 0.0)
        d2 = _split(d2s[h])
        ps_ = _split(p)
        m = dot3(tuple(jnp.concatenate([s, t], axis=0) for s, t in zip(d2, ps_)), d2)
        d2s[h] = m[:n]
        xs.append(p + m[n:])
    for h in heads:
        xs[h] = xs[h] + dot3(_split(xs[h]), _split(d2s[h]))
    b = base_blk
    while b < c:
        off = jnp.logical_and(lower, jnp.logical_and(blk >= b, blk < 2 * b))
        o_splits = [tuple(jnp.where(off, t, zero_bf) for t in s) for s in a_splits]
        x_splits = [_split(x) for x in xs]
        ys = [dot3(o_splits[h], x_splits[h]) for h in heads]
        for h in heads:
            xs[h] = xs[h] - dot3(x_splits[h], _split(ys[h]))
        b *= 2
    for h in heads:
        sol = dot3(_split(xs[h]), _split(rhs[h]))
        u_ref[:, sls[h]] = sol[:, :LANES]
        w_ref[:, sls[h]] = sol[:, LANES:].astype(w_ref.dtype)


def _gdn_prepass(proj, ba, conv_w, alog_row, dtb_row, seq):
    t = proj.shape[0]
    gw = GDN_HEADS * GDN_HEAD_DIM
    n = GDN_PAIR
    ppb = seq // n
    hb = n // BF16_ROWS

    def cur(cb):
        return pl.BlockSpec((n, gw), lambda i: (i, cb))

    def halo(cb):
        return pl.BlockSpec((BF16_ROWS, gw), lambda i: (jnp.maximum(i * hb - 1, 0), cb))

    row_spec = pl.BlockSpec((n, gw), lambda i: (i, 0))
    return pl.pallas_call(
        functools.partial(_gdn_pre_kernel, pairs_per_batch=ppb),
        grid=(t // n,),
        in_specs=[cur(0), cur(1), cur(2), halo(0), halo(1), halo(2),
                  pl.BlockSpec((n, LANES), lambda i: (i, 0)),
                  pl.BlockSpec((GDN_CONV, 3 * gw), lambda i: (0, 0)),
                  pl.BlockSpec((1, LANES), lambda i: (0, 0)),
                  pl.BlockSpec((1, LANES), lambda i: (0, 0))],
        out_specs=[row_spec, row_spec, row_spec, row_spec, row_spec,
                   pl.BlockSpec((1, 2, gw), lambda i: (i, 0, 0))],
        out_shape=[jax.ShapeDtypeStruct((t, gw), F32),
                   jax.ShapeDtypeStruct((t, gw), BF16),
                   jax.ShapeDtypeStruct((t, gw), BF16),
                   jax.ShapeDtypeStruct((t, gw), BF16),
                   jax.ShapeDtypeStruct((t, gw), BF16),
                   jax.ShapeDtypeStruct((t // n, 2, gw), F32)],
        compiler_params=_params("parallel"),
        name="gdn_prepass",
    )(proj, proj, proj, proj, proj, proj, ba, conv_w, alog_row, dtb_row)


def _gdn_scan_kernel(u_ref, w_ref, qd_ref, ai_ref, kdt_ref, gl_ref, z_ref, nw_ref, o_ref, st_ref):
    c = GDN_CHUNK

    @pl.when(pl.program_id(1) == 0)
    def _():
        st_ref[...] = jnp.zeros_like(st_ref)

    zeros = jnp.zeros((c, LANES), F32)
    nw = nw_ref[...]
    heads = range(GDN_HEADS)
    sls = [slice(h * LANES, (h + 1) * LANES) for h in heads]
    states = [st_ref[h] for h in heads]
    for ci in range(2):
        rs = slice(ci * c, (ci + 1) * c)
        sbs = [s.astype(BF16) for s in states]
        ws = [_dot(w_ref[rs, sls[h]], sbs[h]) for h in heads]
        qs = [_dot(qd_ref[rs, sls[h]], sbs[h]) for h in heads]
        vpads = []
        for h in heads:
            vnew = u_ref[rs, sls[h]] - ws[h]
            vpads.append(jnp.concatenate([vnew, zeros] if ci == 0 else [zeros, vnew], axis=0).astype(BF16))
        outs = [qs[h] + _dot(ai_ref[rs, sls[h]], vpads[h]) for h in heads]
        states = [states[h] * gl_ref[0, ci:ci + 1, sls[h]] + _dot(kdt_ref[:, sls[h]], vpads[h]) for h in heads]
        for h in heads:
            o = outs[h]
            z = z_ref[rs, sls[h]].astype(F32)
            o = o * lax.rsqrt(jnp.mean(o * o, axis=-1, keepdims=True) + 1e-6) * nw * _silu(z)
            o_ref[rs, sls[h]] = o.astype(o_ref.dtype)
    for h in heads:
        st_ref[h] = states[h]


def _gdn_scan(u, w, qd, ai, kdt, gl, proj, z_col, norm_w_row, batch, seq):
    t = u.shape[0]
    gw = GDN_HEADS * GDN_HEAD_DIM
    n = GDN_PAIR
    ppb = seq // n
    row_spec = pl.BlockSpec((n, gw), lambda b, i: (b * ppb + i, 0))
    return pl.pallas_call(
        _gdn_scan_kernel,
        grid=(batch, ppb),
        in_specs=[row_spec, row_spec, row_spec, row_spec, row_spec,
                  pl.BlockSpec((1, 2, gw), lambda b, i: (b * ppb + i, 0, 0)),
                  pl.BlockSpec((n, gw), lambda b, i: (b * ppb + i, z_col)),
                  pl.BlockSpec((1, LANES), lambda b, i: (0, 0))],
        out_specs=row_spec,
        out_shape=jax.ShapeDtypeStruct((t, gw), BF16),
        scratch_shapes=[pltpu.VMEM((GDN_HEADS, GDN_HEAD_DIM, GDN_HEAD_DIM), F32)],
        compiler_params=_params("arbitrary", "arbitrary"),
        name="gdn_scan",
    )(u, w, qd, ai, kdt, gl, proj, norm_w_row)


def _merge_kernel(x_ref, a_ref, g_ref, wgs_ref, wgg_ref, wbs_ref, wbg_ref, o_ref):
    x = x_ref[...]
    gate_s = _sigmoid(jnp.dot(x, wgs_ref[...], preferred_element_type=F32))
    gate_g = _sigmoid(jnp.dot(x, wgg_ref[...], preferred_element_type=F32))
    y_s = jnp.dot(a_ref[...], wbs_ref[...], preferred_element_type=F32)
    y_g = jnp.dot(g_ref[...], wbg_ref[...], preferred_element_type=F32)
    o_ref[...] = (gate_s * y_s + gate_g * y_g).astype(o_ref.dtype)


def _merge(xb, attn, gdn, wgs, wgg, wbs, wbg, tm):
    t, d = xb.shape
    ws = attn.shape[1]
    wg = gdn.shape[1]
    tn = wgs.shape[2]

    def wspec(k):
        return pl.BlockSpec((None, k, tn), lambda i, j: (j, 0, 0))

    return pl.pallas_call(
        _merge_kernel,
        grid=(t // tm, d // tn),
        in_specs=[pl.BlockSpec((tm, d), lambda i, j: (i, 0)),
                  pl.BlockSpec((tm, ws), lambda i, j: (i, 0)),
                  pl.BlockSpec((tm, wg), lambda i, j: (i, 0)),
                  wspec(d), wspec(d), wspec(ws), wspec(wg)],
        out_specs=pl.BlockSpec((tm, tn), lambda i, j: (i, j)),
        out_shape=jax.ShapeDtypeStruct((t, d), BF16),
        compiler_params=_params("parallel", "arbitrary"),
        name="branch_merge",
    )(xb, attn, gdn, wgs, wgg, wbs, wbg)


def _mixo_ln_kernel(m_ref, x_ref, w_ref, g_ref, b_ref, o_ref, *, alpha):
    y = alpha * x_ref[...] + jnp.dot(m_ref[...], w_ref[...], preferred_element_type=F32)
    o_ref[...] = _layer_norm(y, g_ref[...], b_ref[...])


def _mixo_ln(mixed, x, w, g, b, alpha, tm):
    t, d = x.shape
    vec = pl.BlockSpec((1, d), lambda i: (0, 0))
    return pl.pallas_call(
        functools.partial(_mixo_ln_kernel, alpha=alpha),
        grid=(t // tm,),
        in_specs=[pl.BlockSpec((tm, d), lambda i: (i, 0)),
                  pl.BlockSpec((tm, d), lambda i: (i, 0)),
                  pl.BlockSpec((d, d), lambda i: (0, 0)),
                  vec, vec],
        out_specs=pl.BlockSpec((tm, d), lambda i: (i, 0)),
        out_shape=jax.ShapeDtypeStruct((t, d), F32),
        compiler_params=_params("parallel"),
        name="mixo_layernorm",
    )(mixed, x, w, g, b)


def _mem_kernel(x_ref, wq_ref, kv_ref, wo_ref, g_ref, b_ref, o_ref, *, alpha):
    x = x_ref[...]
    q = jnp.dot(x.astype(BF16), wq_ref[...], preferred_element_type=F32)
    mw = MEM_HEADS * MEM_HEAD_DIM
    outs = []
    qb = q.astype(BF16)
    scores = [_dot_nt(qb[:, h * MEM_HEAD_DIM:(h + 1) * MEM_HEAD_DIM],
                      kv_ref[:, h * MEM_HEAD_DIM:(h + 1) * MEM_HEAD_DIM]) for h in range(MEM_HEADS)]
    for h in range(MEM_HEADS):
        s = scores[h] * (MEM_HEAD_DIM ** -0.5)
        m = jnp.max(s, axis=-1, keepdims=True)
        e = jnp.exp(s - m)
        d = jnp.sum(e, axis=-1, keepdims=True)
        vh = kv_ref[:, mw + h * MEM_HEAD_DIM:mw + (h + 1) * MEM_HEAD_DIM]
        outs.append(_dot(e, vh) / d)
    o = jnp.concatenate(outs, axis=1)
    y = alpha * x + _dot(o, wo_ref[...])
    o_ref[...] = _layer_norm(y, g_ref[...], b_ref[...])


def _mem_attention_ln(x, wq, kv, wo, g, b, alpha, batch, seq, mem_len, tm):
    t, d = x.shape
    mw = MEM_HEADS * MEM_HEAD_DIM
    tpb = seq // tm
    vec = pl.BlockSpec((1, d), lambda bi, i: (0, 0))
    return pl.pallas_call(
        functools.partial(_mem_kernel, alpha=alpha),
        grid=(batch, tpb),
        in_specs=[pl.BlockSpec((tm, d), lambda bi, i: (bi * tpb + i, 0)),
                  pl.BlockSpec((d, mw), lambda bi, i: (0, 0)),
                  pl.BlockSpec((mem_len, 2 * mw), lambda bi, i: (bi, 0)),
                  pl.BlockSpec((mw, d), lambda bi, i: (0, 0)),
                  vec, vec],
        out_specs=pl.BlockSpec((tm, d), lambda bi, i: (bi * tpb + i, 0)),
        out_shape=jax.ShapeDtypeStruct((t, d), F32),
        compiler_params=_params("parallel", "arbitrary"),
        name="mem_attention_layernorm",
    )(x, wq, kv, wo, g, b)


def _ffn_kernel(x_ref, halo_ref, wg_ref, wu_ref, cg_ref, cu_ref, bg_ref, bu_ref, wd_ref, g_ref, b_ref,
                o_ref, xb_ref, h0_ref, h1_ref, act_ref, *, alpha, tiles_per_batch, n_up, tn, sub, tdn, rows):
    j = pl.program_id(1)
    hr = BF16_ROWS
    h_refs = (h0_ref, h1_ref)

    def up_dots(h_ref):
        xb = xb_ref[...]
        for n in range(tn // sub):
            cs = slice(n * sub, (n + 1) * sub)
            h_ref[:, cs] = jnp.dot(xb, wg_ref[:, cs], preferred_element_type=F32)
            h_ref[:, tn + n * sub:tn + (n + 1) * sub] = jnp.dot(xb, wu_ref[:, cs],
                                                               preferred_element_type=F32)

    def activate(h_ref, slab):
        base = pl.multiple_of(slab * tn, tn)
        tm = act_ref.shape[0]
        for n in range(tn // sub):
            cs = slice(n * sub, (n + 1) * sub)
            us = slice(tn + n * sub, tn + (n + 1) * sub)
            cwg, cwu = cg_ref[:, cs], cu_ref[:, cs]
            bgv, buv = bg_ref[:, cs], bu_ref[:, cs]
            for r0 in range(0, tm, rows):
                def conv(cols, cw, bias):
                    y = bias
                    for k in range(FFN_CONV):
                        tap = h_ref[hr + r0 - k:hr + r0 - k + rows, cols]
                        y = y + cw[FFN_CONV - 1 - k:FFN_CONV - k] * tap
                    return y

                act = _silu(conv(cs, cwg, bgv)) * conv(us, cwu, buv)
                act_ref[r0:r0 + rows, pl.ds(base + n * sub, sub)] = act.astype(BF16)

    @pl.when(j == 0)
    def _():
        first = (pl.program_id(0) % tiles_per_batch) == 0
        xb_ref[:hr, :] = jnp.where(first, 0.0, halo_ref[...]).astype(BF16)
        xb_ref[hr:, :] = x_ref[...].astype(BF16)
        up_dots(h_refs[0])

    steady = jnp.logical_and(j > 0, j < n_up)
    for parity in range(2):
        @pl.when(jnp.logical_and(steady, j % 2 == parity))
        def _():
            activate(h_refs[1 - parity], j - 1)
            up_dots(h_refs[parity])

    @pl.when(j == n_up)
    def _():
        activate(h_refs[(n_up - 1) % 2], n_up - 1)

    @pl.when(j > n_up)
    def _():
        col = pl.multiple_of((j - n_up - 1) * tdn, tdn)
        o_ref[:, pl.ds(col, tdn)] = jnp.dot(act_ref[...], wd_ref[...], preferred_element_type=F32)

    @pl.when(j == pl.num_programs(1) - 1)
    def _():
        y = alpha * x_ref[...] + o_ref[...]
        o_ref[...] = _layer_norm(y, g_ref[...], b_ref[...])


def _ffn_ln(x, wg, wu, cg, cu, bg, bu, wd, g, b, alpha, seq, tm, sub, rows):
    t, d = x.shape
    n_up, _, tn = wg.shape
    n_dn, ff, tdn = wd.shape
    tpb = seq // tm
    hb = tm // BF16_ROWS
    vec = pl.BlockSpec((1, d), lambda i, j: (0, 0))

    def up_tile(i, j):
        return (jnp.minimum(j, n_up - 1), 0, 0)

    def act_col(i, j):
        return (0, jnp.clip(j - 1, 0, n_up - 1))

    return pl.pallas_call(
        functools.partial(_ffn_kernel, alpha=alpha, tiles_per_batch=tpb, n_up=n_up, tn=tn, sub=sub,
                          tdn=tdn, rows=rows),
        grid=(t // tm, n_up + 1 + n_dn),
        in_specs=[pl.BlockSpec((tm, d), lambda i, j: (i, 0)),
                  pl.BlockSpec((BF16_ROWS, d), lambda i, j: (jnp.maximum(i * hb - 1, 0), 0)),
                  pl.BlockSpec((None, d, tn), up_tile),
                  pl.BlockSpec((None, d, tn), up_tile),
                  pl.BlockSpec((FFN_CONV, tn), act_col),
                  pl.BlockSpec((FFN_CONV, tn), act_col),
                  pl.BlockSpec((1, tn), act_col),
                  pl.BlockSpec((1, tn), act_col),
                  pl.BlockSpec((None, ff, tdn), lambda i, j: (jnp.maximum(j - n_up - 1, 0), 0, 0)),
                  vec, vec],
        out_specs=pl.BlockSpec((tm, d), lambda i, j: (i, 0)),
        out_shape=jax.ShapeDtypeStruct((t, d), F32),
        scratch_shapes=[pltpu.VMEM((BF16_ROWS + tm, d), BF16),
                        pltpu.VMEM((BF16_ROWS + tm, 2 * tn), F32),
                        pltpu.VMEM((BF16_ROWS + tm, 2 * tn), F32),
                        pltpu.VMEM((tm, ff), BF16)],
        compiler_params=_params("parallel", "arbitrary"),
        name="ffn_layernorm",
    )(x, x, wg, wu, cg, cu, bg, bu, wd, g, b)


def _pad_cols(a, n):
    return jnp.pad(a, ((0, 0), (0, n - a.shape[1])))


def _layer(x2d, mem2d, batch, seq, mem_len, p, bias_tab, alpha, tiles):
    d = x2d.shape[1]
    sq = SWA_HEADS * SWA_HEAD_DIM
    skv = SWA_KV_HEADS * SWA_HEAD_DIM
    gw = GDN_HEADS * GDN_HEAD_DIM
    w_in = p["w_in"]
    o_gdn = sq + 2 * skv
    o_ba = o_gdn + 4 * gw
    o_gate = o_ba + 2 * GDN_HEADS
    w_proj = jnp.concatenate([w_in[:, o_gdn:o_ba], w_in[:, :o_gdn]], axis=1).astype(BF16)
    w_ba = _pad_cols(w_in[:, o_ba:o_gate], LANES).astype(BF16)
    w_gs = w_in[:, o_gate:o_gate + d].astype(BF16)
    w_gg = w_in[:, o_gate + d:o_gate + 2 * d].astype(BF16)

    proj, xb = _proj(x2d, _tile_major(w_proj, tiles["proj_tn"]), tiles["proj_tm"])
    ba = _matmul(xb, w_ba, tiles["proj_tm"], LANES, F32)

    attn = _swa_attention(proj, (4 * gw) // sq, (4 * gw + sq) // (2 * skv), p["swa_sinks"], bias_tab, seq)

    zpad = jnp.zeros((GDN_HEADS,), F32)
    alog_row = _pad_cols(jnp.concatenate([zpad, p["gdn_a_log"]])[None, :], LANES)
    dtb_row = _pad_cols(jnp.concatenate([zpad, p["gdn_dt_bias"]])[None, :], LANES)
    u, w, qd, ai, kdt, gl = _gdn_prepass(proj, ba, p["gdn_conv_w"], alog_row, dtb_row, seq)
    gdn = _gdn_scan(u, w, qd, ai, kdt, gl, proj, 3, p["gdn_norm_w"][None, :], batch, seq)

    mtn = tiles["merge_tn"]
    mixed = _merge(xb, attn, gdn, _tile_major(w_gs, mtn), _tile_major(w_gg, mtn),
                   _tile_major(p["w_br_swa"].astype(BF16), mtn),
                   _tile_major(p["w_br_gdn"].astype(BF16), mtn), tiles["merge_tm"])
    x1 = _mixo_ln(mixed, x2d, p["w_mix_o"].astype(BF16), p["ln1_g"][None, :], p["ln1_b"][None, :],
                  alpha, tiles["ln_tm"])

    kv = _matmul(mem2d.astype(BF16), p["w_mem_kv"].astype(BF16), mem_len, tiles["kv_tn"], BF16)
    x2 = _mem_attention_ln(x1, p["w_mem_q"].astype(BF16), kv, p["w_mem_o"].astype(BF16),
                           p["ln2_g"][None, :], p["ln2_b"][None, :], alpha, batch, seq, mem_len,
                           tiles["mem_tm"])

    dff = p["w_down"].shape[0]
    tn = tiles["ffn_tn"]
    ffp = -(-dff // tn) * tn
    w_up = p["w_up"]
    cw = p["ffn_conv_w"]
    cb = p["ffn_conv_b"][None, :]
    x3 = _ffn_ln(x2,
                 _tile_major(_pad_cols(w_up[:, :dff], ffp).astype(BF16), tn),
                 _tile_major(_pad_cols(w_up[:, dff:], ffp).astype(BF16), tn),
                 _pad_cols(cw[:, :dff], ffp), _pad_cols(cw[:, dff:], ffp),
                 _pad_cols(cb[:, :dff], ffp), _pad_cols(cb[:, dff:], ffp),
                 _tile_major(jnp.pad(p["w_down"], ((0, ffp - dff), (0, 0))).astype(BF16), tiles["ffn_tdn"]),
                 p["ln3_g"][None, :], p["ln3_b"][None, :], alpha, seq, tiles["ffn_tm"],
                 tiles["ffn_sub"], tiles["ffn_rows"])
    return x3


_TILES = dict(proj_tm=1024, proj_tn=768, merge_tm=1024, merge_tn=512, ln_tm=512, kv_tn=512,
              mem_tm=512, ffn_tm=512, ffn_tn=512, ffn_tdn=512, ffn_sub=256, ffn_rows=32)

_PER_LAYER = ("w_in", "swa_sinks", "gdn_conv_w", "gdn_a_log", "gdn_dt_bias", "gdn_norm_w", "w_br_swa",
              "w_br_gdn", "w_mix_o", "ln1_g", "ln1_b", "w_mem_q", "w_mem_kv", "w_mem_o", "ln2_g", "ln2_b",
              "w_up", "ffn_conv_w", "ffn_conv_b", "w_down", "ln3_g", "ln3_b")


def _forward(x, mem, rel_bias, weights, tiles):
    batch, seq, d = x.shape
    mem_len = mem.shape[1]
    depth = weights["w_in"].shape[0]
    alpha = (2 * depth) ** 0.25
    bias_tab = _swa_bias_table(rel_bias)
    x2d = x.reshape(batch * seq, d)
    mem2d = mem.reshape(batch * mem_len, d)
    for l in range(depth):
        p = {name: weights[name][l] for name in _PER_LAYER}
        x2d = _layer(x2d, mem2d, batch, seq, mem_len, p, bias_tab, alpha, tiles)
    return x2d.reshape(batch, seq, d)


def kernel(x, mem, w_in, rel_bias, swa_sinks, gdn_conv_w, gdn_a_log, gdn_dt_bias, gdn_norm_w, w_br_swa, w_br_gdn, w_mix_o, ln1_g, ln1_b, w_mem_q, w_mem_kv, w_mem_o, ln2_g, ln2_b, w_up, ffn_conv_w, ffn_conv_b, w_down, ln3_g, ln3_b):
    weights = dict(w_in=w_in, swa_sinks=swa_sinks, gdn_conv_w=gdn_conv_w, gdn_a_log=gdn_a_log,
                   gdn_dt_bias=gdn_dt_bias, gdn_norm_w=gdn_norm_w, w_br_swa=w_br_swa, w_br_gdn=w_br_gdn,
                   w_mix_o=w_mix_o, ln1_g=ln1_g, ln1_b=ln1_b, w_mem_q=w_mem_q, w_mem_kv=w_mem_kv,
                   w_mem_o=w_mem_o, ln2_g=ln2_g, ln2_b=ln2_b, w_up=w_up, ffn_conv_w=ffn_conv_w,
                   ffn_conv_b=ffn_conv_b, w_down=w_down, ln3_g=ln3_g, ln3_b=ln3_b)
    return _forward(x, mem, rel_bias, weights, _TILES)
```

```python
import functools
import math

import jax
import jax.numpy as jnp
from jax import lax
from jax.experimental import pallas as pl
from jax.experimental.pallas import tpu as pltpu

F32 = jnp.float32
BF16 = jnp.bfloat16

LANES = 128
BF16_ROWS = 16
VMEM_LIMIT = 56 * 1024 * 1024

SWA_HEADS = 16
SWA_KV_HEADS = 2
SWA_HEAD_DIM = 64
SWA_BLOCK = 128
REL_BUCKETS = 32
REL_MAX_DIST = 128
GDN_HEADS = 8
GDN_HEAD_DIM = 128
GDN_CONV = 4
GDN_CHUNK = 64
GDN_PAIR = 2 * GDN_CHUNK
MEM_HEADS = 4
MEM_HEAD_DIM = 128
FFN_CONV = 3
NORM_EPS = 1e-5
NEG_INF = -1e30


def _params(*sem):
    return pltpu.CompilerParams(dimension_semantics=sem, vmem_limit_bytes=VMEM_LIMIT)


def _dot(a, b):
    return jnp.dot(a.astype(BF16), b.astype(BF16), preferred_element_type=F32)


def _dot_nt(a, b):
    return lax.dot_general(a.astype(BF16), b.astype(BF16), (((1,), (1,)), ((), ())),
                           preferred_element_type=F32)


def _split(a):
    hi = a.astype(BF16)
    lo = (a - hi.astype(F32)).astype(BF16)
    return hi, lo


def _dot3(a, b):
    ah, al = _split(a)
    bh, bl = _split(b)
    d = functools.partial(jnp.dot, preferred_element_type=F32)
    return d(ah, bh) + (d(al, bh) + d(ah, bl))


def _sigmoid(x):
    return 1.0 / (1.0 + jnp.exp(-x))


def _silu(x):
    return x * _sigmoid(x)


def _layer_norm(y, g, b):
    mu = jnp.mean(y, axis=-1, keepdims=True)
    yc = y - mu
    var = jnp.mean(yc * yc, axis=-1, keepdims=True)
    return yc * lax.rsqrt(var + NORM_EPS) * g + b


def _mm_kernel(x_ref, w_ref, o_ref):
    o_ref[...] = jnp.dot(x_ref[...], w_ref[...], preferred_element_type=F32).astype(o_ref.dtype)


def _matmul(x, w, tm, tn, out_dtype):
    m, k = x.shape
    n = w.shape[1]
    assert m % tm == 0 and n % tn == 0
    return pl.pallas_call(
        _mm_kernel,
        grid=(m // tm, n // tn),
        in_specs=[pl.BlockSpec((tm, k), lambda i, j: (i, 0)),
                  pl.BlockSpec((k, tn), lambda i, j: (0, j))],
        out_specs=pl.BlockSpec((tm, tn), lambda i, j: (i, j)),
        out_shape=jax.ShapeDtypeStruct((m, n), out_dtype),
        compiler_params=_params("parallel", "arbitrary"),
        name="matmul",
    )(x, w)


def _proj_kernel(x_ref, w_ref, o_ref, xb_ref):
    @pl.when(pl.program_id(1) == 0)
    def _():
        xb_ref[...] = x_ref[...].astype(xb_ref.dtype)

    o_ref[...] = jnp.dot(xb_ref[...], w_ref[...], preferred_element_type=F32).astype(o_ref.dtype)


def _proj(x, w, tm, tn):
    m, k = x.shape
    n = w.shape[1]
    assert m % tm == 0 and n % tn == 0
    return pl.pallas_call(
        _proj_kernel,
        grid=(m // tm, n // tn),
        in_specs=[pl.BlockSpec((tm, k), lambda i, j: (i, 0)),
                  pl.BlockSpec((k, tn), lambda i, j: (0, j))],
        out_specs=[pl.BlockSpec((tm, tn), lambda i, j: (i, j)),
                   pl.BlockSpec((tm, k), lambda i, j: (i, 0))],
        out_shape=[jax.ShapeDtypeStruct((m, n), BF16), jax.ShapeDtypeStruct((m, k), BF16)],
        compiler_params=_params("parallel", "arbitrary"),
        name="input_proj",
    )(x, w)


def _t5_causal_bucket(dist):
    max_exact = REL_BUCKETS // 2
    d = jnp.maximum(dist, 1).astype(F32)
    large = max_exact + (jnp.log(d / max_exact) / math.log(REL_MAX_DIST / max_exact)
                         * (REL_BUCKETS - max_exact)).astype(jnp.int32)
    large = jnp.minimum(large, REL_BUCKETS - 1)
    return jnp.where(dist < max_exact, dist, large)


def _swa_bias_table(rel_bias):
    nq, nk = SWA_BLOCK, 2 * SWA_BLOCK
    dist = nk - 1 - jnp.arange(nq + nk - 1)
    per_dist = rel_bias.astype(F32)[_t5_causal_bucket(jnp.maximum(dist, 0))]
    per_dist = jnp.where(((dist >= 0) & (dist < SWA_BLOCK))[:, None], per_dist, NEG_INF)
    u = jnp.pad(per_dist.T, ((0, 0), (0, 1)))
    skew = jnp.tile(u, (1, nq))[:, :nq * (nq + nk - 1)].reshape(-1, nq, nq + nk - 1)
    return skew[:, :, nq - 1:]


def _swa_kernel(sink_ref, q_ref, kv_ref, pkv_ref, bias_ref, o_ref, *, blocks_per_batch):
    half = SWA_HEAD_DIM
    first = (pl.program_id(0) % blocks_per_batch) == 0
    kv = jnp.concatenate([pkv_ref[...], kv_ref[...]], axis=0)
    k = kv[:, :LANES]
    v = kv[:, LANES:]
    lo = lax.broadcasted_iota(jnp.int32, k.shape, 1) < half
    zero = jnp.zeros_like(k)

    def swap(t):
        return jnp.concatenate([t[:, half:], t[:, :half]], axis=1)

    def padded(t):
        tr = swap(t)
        return ((jnp.where(lo, t, zero), jnp.where(lo, zero, tr)),
                (jnp.where(lo, tr, zero), jnp.where(lo, zero, t)))

    kpad = padded(k)
    vpad = padded(v)
    col = lax.broadcasted_iota(jnp.int32, (SWA_BLOCK, 2 * SWA_BLOCK), 1)
    no_prev = jnp.logical_and(first, col < SWA_BLOCK)
    out_lo = lax.broadcasted_iota(jnp.int32, (SWA_BLOCK, LANES), 1) < half
    group = SWA_HEADS // SWA_KV_HEADS
    scores = [_dot_nt(q_ref[:, (head // 2) * LANES:(head // 2 + 1) * LANES], kpad[head // group][head % 2])
              for head in range(SWA_HEADS)]
    for p in range(SWA_HEADS // 2):
        acc = None
        inv = []
        for par in range(2):
            head = 2 * p + par
            s = scores[head] * (SWA_HEAD_DIM ** -0.5) + bias_ref[head]
            s = jnp.where(no_prev, NEG_INF, s)
            sink = sink_ref[head]
            m = jnp.maximum(jnp.max(s, axis=-1, keepdims=True), sink)
            e = jnp.exp(s - m)
            d = jnp.sum(e, axis=-1, keepdims=True) + jnp.exp(sink - m)
            pv = _dot(e, vpad[head // group][par])
            acc = pv if acc is None else acc + pv
            inv.append(1.0 / d)
        o_ref[:, p * LANES:(p + 1) * LANES] = (acc * jnp.where(out_lo, inv[0], inv[1])).astype(o_ref.dtype)


def _swa_attention(proj, q_col, kv_col, sinks, bias_tab, seq):
    t = proj.shape[0]
    qw = SWA_HEADS * SWA_HEAD_DIM
    kvw = 2 * SWA_KV_HEADS * SWA_HEAD_DIM
    bpb = seq // SWA_BLOCK
    return pl.pallas_call(
        functools.partial(_swa_kernel, blocks_per_batch=bpb),
        grid=(t // SWA_BLOCK,),
        in_specs=[pl.BlockSpec(memory_space=pltpu.SMEM),
                  pl.BlockSpec((SWA_BLOCK, qw), lambda i: (i, q_col)),
                  pl.BlockSpec((SWA_BLOCK, kvw), lambda i: (i, kv_col)),
                  pl.BlockSpec((SWA_BLOCK, kvw), lambda i: (jnp.maximum(i - 1, 0), kv_col)),
                  pl.BlockSpec((SWA_HEADS, SWA_BLOCK, 2 * SWA_BLOCK), lambda i: (0, 0, 0))],
        out_specs=pl.BlockSpec((SWA_BLOCK, qw), lambda i: (i, 0)),
        out_shape=jax.ShapeDtypeStruct((t, qw), BF16),
        compiler_params=_params("parallel"),
        name="swa_attention",
    )(sinks, proj, proj, proj, bias_tab)


def _gdn_pre_kernel(q_ref, k_ref, v_ref, hq_ref, hk_ref, hv_ref, ba_ref, cw_ref, alog_ref, dtb_ref,
                    u_ref, w_ref, qd_ref, ai_ref, kdt_ref, gl_ref, *, pairs_per_batch):
    n = GDN_PAIR
    c = GDN_CHUNK
    first = (pl.program_id(0) % pairs_per_batch) == 0
    row = lax.broadcasted_iota(jnp.int32, (n, n), 0)
    col = lax.broadcasted_iota(jnp.int32, (n, n), 1)
    same = (row >= c) == (col >= c)
    tril = jnp.logical_and(same, row >= col)
    strict = jnp.logical_and(same, row > col)
    eye = jnp.where(row == col, 1.0, 0.0).astype(F32)
    rowc = lax.broadcasted_iota(jnp.int32, (n, 1), 0)

    ba = ba_ref[...]
    beta_all = _sigmoid(ba)
    x = ba + dtb_ref[...]
    softplus = jnp.maximum(x, 0.0) + jnp.log1p(jnp.exp(-jnp.abs(x)))
    g_all = -jnp.exp(alog_ref[...]) * softplus
    gc_all = _dot3(jnp.where(tril, 1.0, 0.0).astype(F32), g_all)
    gct_all = gc_all.T

    def conv_silu(cur_ref, halo_ref, which, sl):
        cur = cur_ref[:, sl].astype(F32)
        halo = jnp.where(first, 0.0, halo_ref[:, sl].astype(F32))
        ext = jnp.concatenate([halo, cur], axis=0)
        base = which * GDN_HEADS * GDN_HEAD_DIM
        cw = cw_ref[:, base + sl.start:base + sl.stop]
        y = cw[GDN_CONV - 1:GDN_CONV] * cur
        for j in range(1, GDN_CONV):
            y = y + cw[GDN_CONV - 1 - j:GDN_CONV - j] * pltpu.roll(ext, j, axis=0)[BF16_ROWS:]
        return _silu(y)

    def l2norm(t):
        return t * lax.rsqrt(jnp.sum(t * t, axis=-1, keepdims=True) + 1e-6)

    heads = range(GDN_HEADS)
    sls = [slice(h * LANES, (h + 1) * LANES) for h in heads]
    dotf = functools.partial(jnp.dot, preferred_element_type=F32)

    a_mats, rhs = [], []
    for h in heads:
        sl = sls[h]
        q = l2norm(conv_silu(q_ref, hq_ref, 0, sl)) * (GDN_HEAD_DIM ** -0.5)
        k = l2norm(conv_silu(k_ref, hk_ref, 1, sl))
        v = conv_silu(v_ref, hv_ref, 2, sl)
        beta = beta_all[:, h:h + 1]
        gc = gc_all[:, GDN_HEADS + h:GDN_HEADS + h + 1]
        gcr = gct_all[GDN_HEADS + h:GDN_HEADS + h + 1, :]
        kb = k * beta
        kq = _dot_nt(jnp.concatenate([kb, q], axis=0), k)
        decay = jnp.where(tril, jnp.exp(jnp.where(tril, gc - gcr, 0.0)), 0.0)
        a_mats.append(jnp.where(strict, kq[:n] * decay, 0.0))
        egc = jnp.exp(gc)
        rhs.append(jnp.concatenate([v * beta, kb * egc], axis=1))
        gl_e = gc[c - 1:c, :]
        gl_o = gc[n - 1:n, :]
        kd = k * jnp.exp(jnp.where(rowc < c, gl_e, gl_o) - gc)
        qd_ref[:, sl] = (q * egc).astype(qd_ref.dtype)
        ai_ref[:, sl] = (kq[n:] * decay).astype(ai_ref.dtype)
        kdt_ref[:, sl] = kd.T.astype(kdt_ref.dtype)
        gl_ref[0, 0:1, sl] = jnp.broadcast_to(jnp.exp(gl_e), (1, LANES))
        gl_ref[0, 1:2, sl] = jnp.broadcast_to(jnp.exp(gl_o), (1, LANES))

    def dot3(a_split, b_split):
        (ah, al), (bh, bl) = a_split, b_split
        return dotf(jnp.concatenate([ah, al, ah], axis=1), jnp.concatenate([bh, bh, bl], axis=0))

    blk = row ^ col
    lower = row > col
    base_blk = 8
    diag = jnp.logical_and(lower, blk < base_blk)
    zero_bf = jnp.zeros((n, n), BF16)
    a_splits = [_split(a) for a in a_mats]
    d_splits = [tuple(jnp.where(diag, t, zero_bf) for t in s) for s in a_splits]
    d2s = [dot3(d, d) for d in d_splits]
    xs = []
    for h in heads:
        p = eye - jnp.where(diag, a_mats[h], 0.0)
        d2 = _split(d2s[h])
        ps_ = _split(p)
        m = dot3(tuple(jnp.concatenate([s, t], axis=0) for s, t in zip(d2, ps_)), d2)
        d2s[h] = m[:n]
        xs.append(p + m[n:])
    for h in heads:
        xs[h] = xs[h] + dot3(_split(xs[h]), _split(d2s[h]))
    b = base_blk
    while b < c:
        off = jnp.logical_and(lower, jnp.logical_and(blk >= b, blk < 2 * b))
        o_splits = [tuple(jnp.where(off, t, zero_bf) for t in s) for s in a_splits]
        x_splits = [_split(x) for x in xs]
        ys = [dot3(o_splits[h], x_splits[h]) for h in heads]
        for h in heads:
            xs[h] = xs[h] - dot3(x_splits[h], _split(ys[h]))
        b *= 2
    for h in heads:
        sol = dot3(_split(xs[h]), _split(rhs[h]))
        u_ref[:, sls[h]] = sol[:, :LANES]
        w_ref[:, sls[h]] = sol[:, LANES:].astype(w_ref.dtype)


def _gdn_prepass(proj, ba, conv_w, alog_row, dtb_row, seq):
    t = proj.shape[0]
    gw = GDN_HEADS * GDN_HEAD_DIM
    n = GDN_PAIR
    ppb = seq // n
    hb = n // BF16_ROWS

    def cur(cb):
        return pl.BlockSpec((n, gw), lambda i: (i, cb))

    def halo(cb):
        return pl.BlockSpec((BF16_ROWS, gw), lambda i: (jnp.maximum(i * hb - 1, 0), cb))

    row_spec = pl.BlockSpec((n, gw), lambda i: (i, 0))
    return pl.pallas_call(
        functools.partial(_gdn_pre_kernel, pairs_per_batch=ppb),
        grid=(t // n,),
        in_specs=[cur(0), cur(1), cur(2), halo(0), halo(1), halo(2),
                  pl.BlockSpec((n, LANES), lambda i: (i, 0)),
                  pl.BlockSpec((GDN_CONV, 3 * gw), lambda i: (0, 0)),
                  pl.BlockSpec((1, LANES), lambda i: (0, 0)),
                  pl.BlockSpec((1, LANES), lambda i: (0, 0))],
        out_specs=[row_spec, row_spec, row_spec, row_spec, row_spec,
                   pl.BlockSpec((1, 2, gw), lambda i: (i, 0, 0))],
        out_shape=[jax.ShapeDtypeStruct((t, gw), F32),
                   jax.ShapeDtypeStruct((t, gw), BF16),
                   jax.ShapeDtypeStruct((t, gw), BF16),
                   jax.ShapeDtypeStruct((t, gw), BF16),
                   jax.ShapeDtypeStruct((t, gw), BF16),
                   jax.ShapeDtypeStruct((t // n, 2, gw), F32)],
        compiler_params=_params("parallel"),
        name="gdn_prepass",
    )(proj, proj, proj, proj, proj, proj, ba, conv_w, alog_row, dtb_row)


def _gdn_scan_kernel(u_ref, w_ref, qd_ref, ai_ref, kdt_ref, gl_ref, z_ref, nw_ref, o_ref, st_ref):
    c = GDN_CHUNK

    @pl.when(pl.program_id(1) == 0)
    def _():
        st_ref[...] = jnp.zeros_like(st_ref)

    zeros = jnp.zeros((c, LANES), F32)
    nw = nw_ref[...]
    heads = range(GDN_HEADS)
    sls = [slice(h * LANES, (h + 1) * LANES) for h in heads]
    states = [st_ref[h] for h in heads]
    for ci in range(2):
        rs = slice(ci * c, (ci + 1) * c)
        sbs = [s.astype(BF16) for s in states]
        ws = [_dot(w_ref[rs, sls[h]], sbs[h]) for h in heads]
        qs = [_dot(qd_ref[rs, sls[h]], sbs[h]) for h in heads]
        vpads = []
        for h in heads:
            vnew = u_ref[rs, sls[h]] - ws[h]
            vpads.append(jnp.concatenate([vnew, zeros] if ci == 0 else [zeros, vnew], axis=0).astype(BF16))
        outs = [qs[h] + _dot(ai_ref[rs, sls[h]], vpads[h]) for h in heads]
        states = [states[h] * gl_ref[0, ci:ci + 1, sls[h]] + _dot(kdt_ref[:, sls[h]], vpads[h]) for h in heads]
        for h in heads:
            o = outs[h]
            z = z_ref[rs, sls[h]].astype(F32)
            o = o * lax.rsqrt(jnp.mean(o * o, axis=-1, keepdims=True) + 1e-6) * nw * _silu(z)
            o_ref[rs, sls[h]] = o.astype(o_ref.dtype)
    for h in heads:
        st_ref[h] = states[h]


def _gdn_scan(u, w, qd, ai, kdt, gl, proj, z_col, norm_w_row, batch, seq):
    t = u.shape[0]
    gw = GDN_HEADS * GDN_HEAD_DIM
    n = GDN_PAIR
    ppb = seq // n
    row_spec = pl.BlockSpec((n, gw), lambda b, i: (b * ppb + i, 0))
    return pl.pallas_call(
        _gdn_scan_kernel,
        grid=(batch, ppb),
        in_specs=[row_spec, row_spec, row_spec, row_spec, row_spec,
                  pl.BlockSpec((1, 2, gw), lambda b, i: (b * ppb + i, 0, 0)),
                  pl.BlockSpec((n, gw), lambda b, i: (b * ppb + i, z_col)),
                  pl.BlockSpec((1, LANES), lambda b, i: (0, 0))],
        out_specs=row_spec,
        out_shape=jax.ShapeDtypeStruct((t, gw), BF16),
        scratch_shapes=[pltpu.VMEM((GDN_HEADS, GDN_HEAD_DIM, GDN_HEAD_DIM), F32)],
        compiler_params=_params("arbitrary", "arbitrary"),
        name="gdn_scan",
    )(u, w, qd, ai, kdt, gl, proj, norm_w_row)


def _merge_kernel(x_ref, a_ref, g_ref, wgs_ref, wgg_ref, wbs_ref, wbg_ref, o_ref):
    x = x_ref[...]
    gate_s = _sigmoid(jnp.dot(x, wgs_ref[...], preferred_element_type=F32))
    gate_g = _sigmoid(jnp.dot(x, wgg_ref[...], preferred_element_type=F32))
    y_s = jnp.dot(a_ref[...], wbs_ref[...], preferred_element_type=F32)
    y_g = jnp.dot(g_ref[...], wbg_ref[...], preferred_element_type=F32)
    o_ref[...] = (gate_s * y_s + gate_g * y_g).astype(o_ref.dtype)


def _merge(xb, attn, gdn, wgs, wgg, wbs, wbg, tm, tn):
    t, d = xb.shape
    ws = attn.shape[1]
    wg = gdn.shape[1]

    def wspec(k):
        return pl.BlockSpec((k, tn), lambda i, j: (0, j))

    return pl.pallas_call(
        _merge_kernel,
        grid=(t // tm, d // tn),
        in_specs=[pl.BlockSpec((tm, d), lambda i, j: (i, 0)),
                  pl.BlockSpec((tm, ws), lambda i, j: (i, 0)),
                  pl.BlockSpec((tm, wg), lambda i, j: (i, 0)),
                  wspec(d), wspec(d), wspec(ws), wspec(wg)],
        out_specs=pl.BlockSpec((tm, tn), lambda i, j: (i, j)),
        out_shape=jax.ShapeDtypeStruct((t, d), BF16),
        compiler_params=_params("parallel", "arbitrary"),
        name="branch_merge",
    )(xb, attn, gdn, wgs, wgg, wbs, wbg)


def _mixo_ln_kernel(m_ref, x_ref, w_ref, g_ref, b_ref, o_ref, *, alpha):
    y = alpha * x_ref[...] + jnp.dot(m_ref[...], w_ref[...], preferred_element_type=F32)
    o_ref[...] = _layer_norm(y, g_ref[...], b_ref[...])


def _mixo_ln(mixed, x, w, g, b, alpha, tm):
    t, d = x.shape
    vec = pl.BlockSpec((1, d), lambda i: (0, 0))
    return pl.pallas_call(
        functools.partial(_mixo_ln_kernel, alpha=alpha),
        grid=(t // tm,),
        in_specs=[pl.BlockSpec((tm, d), lambda i: (i, 0)),
                  pl.BlockSpec((tm, d), lambda i: (i, 0)),
                  pl.BlockSpec((d, d), lambda i: (0, 0)),
                  vec, vec],
        out_specs=pl.BlockSpec((tm, d), lambda i: (i, 0)),
        out_shape=jax.ShapeDtypeStruct((t, d), F32),
        compiler_params=_params("parallel"),
        name="mixo_layernorm",
    )(mixed, x, w, g, b)


def _mem_kernel(x_ref, wq_ref, kv_ref, wo_ref, g_ref, b_ref, o_ref, *, alpha):
    x = x_ref[...]
    q = jnp.dot(x.astype(BF16), wq_ref[...], preferred_element_type=F32)
    mw = MEM_HEADS * MEM_HEAD_DIM
    outs = []
    qb = q.astype(BF16)
    scores = [_dot_nt(qb[:, h * MEM_HEAD_DIM:(h + 1) * MEM_HEAD_DIM],
                      kv_ref[:, h * MEM_HEAD_DIM:(h + 1) * MEM_HEAD_DIM]) for h in range(MEM_HEADS)]
    for h in range(MEM_HEADS):
        s = scores[h] * (MEM_HEAD_DIM ** -0.5)
        m = jnp.max(s, axis=-1, keepdims=True)
        e = jnp.exp(s - m)
        d = jnp.sum(e, axis=-1, keepdims=True)
        vh = kv_ref[:, mw + h * MEM_HEAD_DIM:mw + (h + 1) * MEM_HEAD_DIM]
        outs.append(_dot(e, vh) / d)
    o = jnp.concatenate(outs, axis=1)
    y = alpha * x + _dot(o, wo_ref[...])
    o_ref[...] = _layer_norm(y, g_ref[...], b_ref[...])


def _mem_attention_ln(x, wq, kv, wo, g, b, alpha, batch, seq, mem_len, tm):
    t, d = x.shape
    mw = MEM_HEADS * MEM_HEAD_DIM
    tpb = seq // tm
    vec = pl.BlockSpec((1, d), lambda bi, i: (0, 0))
    return pl.pallas_call(
        functools.partial(_mem_kernel, alpha=alpha),
        grid=(batch, tpb),
        in_specs=[pl.BlockSpec((tm, d), lambda bi, i: (bi * tpb + i, 0)),
                  pl.BlockSpec((d, mw), lambda bi, i: (0, 0)),
                  pl.BlockSpec((mem_len, 2 * mw), lambda bi, i: (bi, 0)),
                  pl.BlockSpec((mw, d), lambda bi, i: (0, 0)),
                  vec, vec],
        out_specs=pl.BlockSpec((tm, d), lambda bi, i: (bi * tpb + i, 0)),
        out_shape=jax.ShapeDtypeStruct((t, d), F32),
        compiler_params=_params("parallel", "arbitrary"),
        name="mem_attention_layernorm",
    )(x, wq, kv, wo, g, b)


def _ffn_kernel(x_ref, halo_ref, wg_ref, wu_ref, cg_ref, cu_ref, bg_ref, bu_ref, wd_ref, g_ref, b_ref,
                o_ref, xb_ref, acc_ref, *, alpha, tiles_per_batch):
    j = pl.program_id(1)
    hr = BF16_ROWS

    @pl.when(j == 0)
    def _():
        first = (pl.program_id(0) % tiles_per_batch) == 0
        xb_ref[:hr, :] = jnp.where(first, 0.0, halo_ref[...]).astype(BF16)
        xb_ref[hr:, :] = x_ref[...].astype(BF16)
        acc_ref[...] = jnp.zeros_like(acc_ref)

    xb = xb_ref[...]

    def conv(w_ref, cw_ref, bias_ref):
        h = jnp.dot(xb, w_ref[...], preferred_element_type=F32)
        cw = cw_ref[...]
        y = cw[FFN_CONV - 1:FFN_CONV] * h[hr:] + bias_ref[...]
        for k in range(1, FFN_CONV):
            y = y + cw[FFN_CONV - 1 - k:FFN_CONV - k] * pltpu.roll(h, k, axis=0)[hr:]
        return y

    act = _silu(conv(wg_ref, cg_ref, bg_ref)) * conv(wu_ref, cu_ref, bu_ref)
    acc_ref[...] += _dot(act, wd_ref[...])

    @pl.when(j == pl.num_programs(1) - 1)
    def _():
        y = alpha * x_ref[...] + acc_ref[...]
        o_ref[...] = _layer_norm(y, g_ref[...], b_ref[...])


def _ffn_ln(x, wg, wu, cg, cu, bg, bu, wd, g, b, alpha, seq, tm, tn):
    t, d = x.shape
    ff = wg.shape[1]
    tpb = seq // tm
    hb = tm // BF16_ROWS
    vec = pl.BlockSpec((1, d), lambda i, j: (0, 0))
    return pl.pallas_call(
        functools.partial(_ffn_kernel, alpha=alpha, tiles_per_batch=tpb),
        grid=(t // tm, ff // tn),
        in_specs=[pl.BlockSpec((tm, d), lambda i, j: (i, 0)),
                  pl.BlockSpec((BF16_ROWS, d), lambda i, j: (jnp.maximum(i * hb - 1, 0), 0)),
                  pl.BlockSpec((d, tn), lambda i, j: (0, j)),
                  pl.BlockSpec((d, tn), lambda i, j: (0, j)),
                  pl.BlockSpec((FFN_CONV, tn), lambda i, j: (0, j)),
                  pl.BlockSpec((FFN_CONV, tn), lambda i, j: (0, j)),
                  pl.BlockSpec((1, tn), lambda i, j: (0, j)),
                  pl.BlockSpec((1, tn), lambda i, j: (0, j)),
                  pl.BlockSpec((tn, d), lambda i, j: (j, 0)),
                  vec, vec],
        out_specs=pl.BlockSpec((tm, d), lambda i, j: (i, 0)),
        out_shape=jax.ShapeDtypeStruct((t, d), F32),
        scratch_shapes=[pltpu.VMEM((BF16_ROWS + tm, d), BF16),
                        pltpu.VMEM((tm, d), F32)],
        compiler_params=_params("parallel", "arbitrary"),
        name="ffn_layernorm",
    )(x, x, wg, wu, cg, cu, bg, bu, wd, g, b)


def _pad_cols(a, n):
    return jnp.pad(a, ((0, 0), (0, n - a.shape[1])))


def _layer(x2d, mem2d, batch, seq, mem_len, p, bias_tab, alpha, tiles):
    d = x2d.shape[1]
    sq = SWA_HEADS * SWA_HEAD_DIM
    skv = SWA_KV_HEADS * SWA_HEAD_DIM
    gw = GDN_HEADS * GDN_HEAD_DIM
    w_in = p["w_in"]
    o_gdn = sq + 2 * skv
    o_ba = o_gdn + 4 * gw
    o_gate = o_ba + 2 * GDN_HEADS
    w_proj = jnp.concatenate([w_in[:, o_gdn:o_ba], w_in[:, :o_gdn]], axis=1).astype(BF16)
    w_ba = _pad_cols(w_in[:, o_ba:o_gate], LANES).astype(BF16)
    w_gs = w_in[:, o_gate:o_gate + d].astype(BF16)
    w_gg = w_in[:, o_gate + d:o_gate + 2 * d].astype(BF16)

    proj, xb = _proj(x2d, w_proj, tiles["proj_tm"], tiles["proj_tn"])
    ba = _matmul(xb, w_ba, tiles["proj_tm"], LANES, F32)

    attn = _swa_attention(proj, (4 * gw) // sq, (4 * gw + sq) // (2 * skv), p["swa_sinks"], bias_tab, seq)

    zpad = jnp.zeros((GDN_HEADS,), F32)
    alog_row = _pad_cols(jnp.concatenate([zpad, p["gdn_a_log"]])[None, :], LANES)
    dtb_row = _pad_cols(jnp.concatenate([zpad, p["gdn_dt_bias"]])[None, :], LANES)
    u, w, qd, ai, kdt, gl = _gdn_prepass(proj, ba, p["gdn_conv_w"], alog_row, dtb_row, seq)
    gdn = _gdn_scan(u, w, qd, ai, kdt, gl, proj, 3, p["gdn_norm_w"][None, :], batch, seq)

    mixed = _merge(xb, attn, gdn, w_gs, w_gg, p["w_br_swa"].astype(BF16), p["w_br_gdn"].astype(BF16),
                   tiles["merge_tm"], tiles["merge_tn"])
    x1 = _mixo_ln(mixed, x2d, p["w_mix_o"].astype(BF16), p["ln1_g"][None, :], p["ln1_b"][None, :],
                  alpha, tiles["ln_tm"])

    kv = _matmul(mem2d.astype(BF16), p["w_mem_kv"].astype(BF16), mem_len, tiles["kv_tn"], BF16)
    x2 = _mem_attention_ln(x1, p["w_mem_q"].astype(BF16), kv, p["w_mem_o"].astype(BF16),
                           p["ln2_g"][None, :], p["ln2_b"][None, :], alpha, batch, seq, mem_len,
                           tiles["mem_tm"])

    dff = p["w_down"].shape[0]
    tn = tiles["ffn_tn"]
    ffp = -(-dff // tn) * tn
    w_up = p["w_up"]
    cw = p["ffn_conv_w"]
    cb = p["ffn_conv_b"][None, :]
    x3 = _ffn_ln(x2,
                 _pad_cols(w_up[:, :dff], ffp).astype(BF16), _pad_cols(w_up[:, dff:], ffp).astype(BF16),
                 _pad_cols(cw[:, :dff], ffp), _pad_cols(cw[:, dff:], ffp),
                 _pad_cols(cb[:, :dff], ffp), _pad_cols(cb[:, dff:], ffp),
                 jnp.pad(p["w_down"], ((0, ffp - dff), (0, 0))).astype(BF16),
                 p["ln3_g"][None, :], p["ln3_b"][None, :], alpha, seq, tiles["ffn_tm"], tn)
    return x3


_TILES = dict(proj_tm=1024, proj_tn=1792, merge_tm=1024, merge_tn=1024, ln_tm=512, kv_tn=512,
              mem_tm=512, ffn_tm=512, ffn_tn=512)

_PER_LAYER = ("w_in", "swa_sinks", "gdn_conv_w", "gdn_a_log", "gdn_dt_bias", "gdn_norm_w", "w_br_swa",
              "w_br_gdn", "w_mix_o", "ln1_g", "ln1_b", "w_mem_q", "w_mem_kv", "w_mem_o", "ln2_g", "ln2_b",
              "w_up", "ffn_conv_w", "ffn_conv_b", "w_down", "ln3_g", "ln3_b")


def _forward(x, mem, rel_bias, weights, tiles):
    batch, seq, d = x.shape
    mem_len = mem.shape[1]
    depth = weights["w_in"].shape[0]
    alpha = (2 * depth) ** 0.25
    bias_tab = _swa_bias_table(rel_bias)
    x2d = x.reshape(batch * seq, d)
    mem2d = mem.reshape(batch * mem_len, d)
    for l in range(depth):
        p = {name: weights[name][l] for name in _PER_LAYER}
        x2d = _layer(x2d, mem2d, batch, seq, mem_len, p, bias_tab, alpha, tiles)
    return x2d.reshape(batch, seq, d)


def kernel(x, mem, w_in, rel_bias, swa_sinks, gdn_conv_w, gdn_a_log, gdn_dt_bias, gdn_norm_w, w_br_swa, w_br_gdn, w_mix_o, ln1_g, ln1_b, w_mem_q, w_mem_kv, w_mem_o, ln2_g, ln2_b, w_up, ffn_conv_w, ffn_conv_b, w_down, ln3_g, ln3_b):
    weights = dict(w_in=w_in, swa_sinks=swa_sinks, gdn_conv_w=gdn_conv_w, gdn_a_log=gdn_a_log,
                   gdn_dt_bias=gdn_dt_bias, gdn_norm_w=gdn_norm_w, w_br_swa=w_br_swa, w_br_gdn=w_br_gdn,
                   w_mix_o=w_mix_o, ln1_g=ln1_g, ln1_b=ln1_b, w_mem_q=w_mem_q, w_mem_kv=w_mem_kv,
                   w_mem_o=w_mem_o, ln2_g=ln2_g, ln2_b=ln2_b, w_up=w_up, ffn_conv_w=ffn_conv_w,
                   ffn_conv_b=ffn_conv_b, w_down=w_down, ln3_g=ln3_g, ln3_b=ln3_b)
    return _forward(x, mem, rel_bias, weights, _TILES)
```

```python
import functools
import math

import jax
import jax.numpy as jnp
from jax import lax
from jax.experimental import pallas as pl
from jax.experimental.pallas import tpu as pltpu

F32 = jnp.float32
BF16 = jnp.bfloat16

LANES = 128
BF16_ROWS = 16
VMEM_LIMIT = 56 * 1024 * 1024

SWA_HEADS = 16
SWA_KV_HEADS = 2
SWA_HEAD_DIM = 64
SWA_BLOCK = 128
REL_BUCKETS = 32
REL_MAX_DIST = 128
GDN_HEADS = 8
GDN_HEAD_DIM = 128
GDN_CONV = 4
GDN_CHUNK = 64
GDN_PAIR = 2 * GDN_CHUNK
MEM_HEADS = 4
MEM_HEAD_DIM = 128
FFN_CONV = 3
NORM_EPS = 1e-5
NEG_INF = -1e30


def _params(*sem):
    return pltpu.CompilerParams(dimension_semantics=sem, vmem_limit_bytes=VMEM_LIMIT)


def _dot(a, b):
    return jnp.dot(a.astype(BF16), b.astype(BF16), preferred_element_type=F32)


def _dot_nt(a, b):
    return lax.dot_general(a.astype(BF16), b.astype(BF16), (((1,), (1,)), ((), ())),
                           preferred_element_type=F32)


def _split(a):
    hi = a.astype(BF16)
    lo = (a - hi.astype(F32)).astype(BF16)
    return hi, lo


def _dot3(a, b):
    ah, al = _split(a)
    bh, bl = _split(b)
    d = functools.partial(jnp.dot, preferred_element_type=F32)
    return d(ah, bh) + (d(al, bh) + d(ah, bl))


def _sigmoid(x):
    return 1.0 / (1.0 + jnp.exp(-x))


def _silu(x):
    return x * _sigmoid(x)


def _layer_norm(y, g, b):
    mu = jnp.mean(y, axis=-1, keepdims=True)
    yc = y - mu
    var = jnp.mean(yc * yc, axis=-1, keepdims=True)
    return yc * lax.rsqrt(var + NORM_EPS) * g + b


def _mm_kernel(x_ref, w_ref, o_ref):
    o_ref[...] = jnp.dot(x_ref[...], w_ref[...], preferred_element_type=F32).astype(o_ref.dtype)


def _matmul(x, w, tm, tn, out_dtype):
    m, k = x.shape
    n = w.shape[1]
    assert m % tm == 0 and n % tn == 0
    return pl.pallas_call(
        _mm_kernel,
        grid=(m // tm, n // tn),
        in_specs=[pl.BlockSpec((tm, k), lambda i, j: (i, 0)),
                  pl.BlockSpec((k, tn), lambda i, j: (0, j))],
        out_specs=pl.BlockSpec((tm, tn), lambda i, j: (i, j)),
        out_shape=jax.ShapeDtypeStruct((m, n), out_dtype),
        compiler_params=_params("parallel", "arbitrary"),
        name="matmul",
    )(x, w)


def _proj_kernel(x_ref, w_ref, ws_ref, o_ref, xb_ref, os_ref):
    @pl.when(pl.program_id(1) == 0)
    def _():
        xb_ref[...] = x_ref[...].astype(xb_ref.dtype)
        os_ref[...] = jnp.dot(xb_ref[...], ws_ref[...], preferred_element_type=F32)

    o_ref[...] = jnp.dot(xb_ref[...], w_ref[...], preferred_element_type=F32).astype(o_ref.dtype)


def _proj(x, w, w_small, tm, tn):
    m, k = x.shape
    n = w.shape[1]
    ns = w_small.shape[1]
    assert m % tm == 0 and n % tn == 0
    return pl.pallas_call(
        _proj_kernel,
        grid=(m // tm, n // tn),
        in_specs=[pl.BlockSpec((tm, k), lambda i, j: (i, 0)),
                  pl.BlockSpec((k, tn), lambda i, j: (0, j)),
                  pl.BlockSpec((k, ns), lambda i, j: (0, 0))],
        out_specs=[pl.BlockSpec((tm, tn), lambda i, j: (i, j)),
                   pl.BlockSpec((tm, k), lambda i, j: (i, 0)),
                   pl.BlockSpec((tm, ns), lambda i, j: (i, 0))],
        out_shape=[jax.ShapeDtypeStruct((m, n), BF16), jax.ShapeDtypeStruct((m, k), BF16),
                   jax.ShapeDtypeStruct((m, ns), F32)],
        compiler_params=_params("parallel", "arbitrary"),
        name="input_proj",
    )(x, w, w_small)


def _t5_causal_bucket(dist):
    max_exact = REL_BUCKETS // 2
    d = jnp.maximum(dist, 1).astype(F32)
    large = max_exact + (jnp.log(d / max_exact) / math.log(REL_MAX_DIST / max_exact)
                         * (REL_BUCKETS - max_exact)).astype(jnp.int32)
    large = jnp.minimum(large, REL_BUCKETS - 1)
    return jnp.where(dist < max_exact, dist, large)


def _swa_bias_table(rel_bias):
    nq, nk = SWA_BLOCK, 2 * SWA_BLOCK
    dist = nk - 1 - jnp.arange(nq + nk - 1)
    per_dist = rel_bias.astype(F32)[_t5_causal_bucket(jnp.maximum(dist, 0))]
    per_dist = jnp.where(((dist >= 0) & (dist < SWA_BLOCK))[:, None], per_dist, NEG_INF)
    u = jnp.pad(per_dist.T, ((0, 0), (0, 1)))
    skew = jnp.tile(u, (1, nq))[:, :nq * (nq + nk - 1)].reshape(-1, nq, nq + nk - 1)
    table = skew[:, :, nq - 1:]
    no_prev = jnp.where(jnp.arange(nk) < nq, NEG_INF, table)
    return jnp.stack([table, no_prev])


def _swa_kernel(sink_ref, q_ref, kv_ref, pkv_ref, bias0_ref, bias_ref, o_ref, *, blocks):
    half = SWA_HEAD_DIM
    kv = jnp.concatenate([pkv_ref[...], kv_ref[...]], axis=0)
    k = kv[:, :LANES]
    v = kv[:, LANES:]
    lo = lax.broadcasted_iota(jnp.int32, k.shape, 1) < half
    zero = jnp.zeros_like(k)

    def swap(t):
        return jnp.concatenate([t[:, half:], t[:, :half]], axis=1)

    def padded(t):
        tr = swap(t)
        return ((jnp.where(lo, t, zero), jnp.where(lo, zero, tr)),
                (jnp.where(lo, tr, zero), jnp.where(lo, zero, t)))

    kpad = padded(k)
    vpad = padded(v)
    out_lo = lax.broadcasted_iota(jnp.int32, (SWA_BLOCK, LANES), 1) < half
    group = SWA_HEADS // SWA_KV_HEADS
    for blk in range(blocks):
        rows = slice(blk * SWA_BLOCK, (blk + 1) * SWA_BLOCK)
        band = slice(blk * SWA_BLOCK, (blk + 2) * SWA_BLOCK)
        table_ref = bias0_ref if blk == 0 else bias_ref
        scores = [_dot_nt(q_ref[rows, (head // 2) * LANES:(head // 2 + 1) * LANES],
                          kpad[head // group][head % 2][band]) for head in range(SWA_HEADS)]
        es, inv = [], []
        for head in range(SWA_HEADS):
            s = scores[head] + table_ref[head]
            sink = sink_ref[head]
            m = jnp.maximum(jnp.max(s, axis=-1, keepdims=True), sink)
            e = jnp.exp(s - m)
            inv.append(1.0 / (jnp.sum(e, axis=-1, keepdims=True) + jnp.exp(sink - m)))
            es.append(e.astype(BF16))
        for p in range(SWA_HEADS // 2):
            acc = (_dot(es[2 * p], vpad[(2 * p) // group][0][band])
                   + _dot(es[2 * p + 1], vpad[(2 * p + 1) // group][1][band]))
            o_ref[rows, p * LANES:(p + 1) * LANES] = (
                acc * jnp.where(out_lo, inv[2 * p], inv[2 * p + 1])).astype(o_ref.dtype)


def _swa_attention(proj, q_col, kv_col, sinks, bias_tabs, seq, blocks):
    t = proj.shape[0]
    qw = SWA_HEADS * SWA_HEAD_DIM
    kvw = 2 * SWA_KV_HEADS * SWA_HEAD_DIM
    tq = blocks * SWA_BLOCK
    spb = seq // tq
    table = (None, SWA_HEADS, SWA_BLOCK, 2 * SWA_BLOCK)
    return pl.pallas_call(
        functools.partial(_swa_kernel, blocks=blocks),
        grid=(t // tq,),
        in_specs=[pl.BlockSpec(memory_space=pltpu.SMEM),
                  pl.BlockSpec((tq, qw), lambda i: (i, q_col)),
                  pl.BlockSpec((tq, kvw), lambda i: (i, kv_col)),
                  pl.BlockSpec((SWA_BLOCK, kvw), lambda i: (jnp.maximum(i * blocks - 1, 0), kv_col)),
                  pl.BlockSpec(table, lambda i: (jnp.where(i % spb == 0, 1, 0), 0, 0, 0)),
                  pl.BlockSpec(table, lambda i: (0, 0, 0, 0))],
        out_specs=pl.BlockSpec((tq, qw), lambda i: (i, 0)),
        out_shape=jax.ShapeDtypeStruct((t, qw), BF16),
        compiler_params=_params("parallel"),
        name="swa_attention",
    )(sinks, proj, proj, proj, bias_tabs, bias_tabs)


def _gdn_pre_kernel(q_ref, k_ref, v_ref, hq_ref, hk_ref, hv_ref, ba_ref, cw_ref, alog_ref, dtb_ref,
                    u_ref, w_ref, qd_ref, ai_ref, kdt_ref, gl_ref, *, pairs_per_batch):
    n = GDN_PAIR
    c = GDN_CHUNK
    first = (pl.program_id(0) % pairs_per_batch) == 0
    row = lax.broadcasted_iota(jnp.int32, (n, n), 0)
    col = lax.broadcasted_iota(jnp.int32, (n, n), 1)
    same = (row >= c) == (col >= c)
    tril = jnp.logical_and(same, row >= col)
    rowc = lax.broadcasted_iota(jnp.int32, (n, 1), 0)
    crow = lax.broadcasted_iota(jnp.int32, (c, n), 0)
    clane = lax.broadcasted_iota(jnp.int32, (c, n), 1)
    first_chunk = clane < c
    ccol = clane & (c - 1)
    ctril = crow >= ccol
    clower = crow > ccol
    cblk = crow ^ ccol
    ceye = jnp.where(crow == ccol, 1.0, 0.0).astype(F32)

    def compact(m):
        return jnp.where(first_chunk, m[:c], m[c:])

    def expand(m):
        z = jnp.zeros_like(m)
        return jnp.concatenate([jnp.where(first_chunk, m, z), jnp.where(first_chunk, z, m)], axis=0)

    ba = ba_ref[...]
    beta_all = _sigmoid(ba)
    x = ba + dtb_ref[...]
    softplus = jnp.maximum(x, 0.0) + jnp.log1p(jnp.exp(-jnp.abs(x)))
    g_all = -jnp.exp(alog_ref[...]) * softplus
    gc_all = _dot3(jnp.where(tril, 1.0, 0.0).astype(F32), g_all)
    gct_all = gc_all.T

    def conv_silu(cur_ref, halo_ref, which, sl):
        cur = cur_ref[:, sl].astype(F32)
        halo = jnp.where(first, 0.0, halo_ref[:, sl].astype(F32))
        ext = jnp.concatenate([halo, cur], axis=0)
        base = which * GDN_HEADS * GDN_HEAD_DIM
        cw = cw_ref[:, base + sl.start:base + sl.stop]
        y = cw[GDN_CONV - 1:GDN_CONV] * cur
        for j in range(1, GDN_CONV):
            y = y + cw[GDN_CONV - 1 - j:GDN_CONV - j] * pltpu.roll(ext, j, axis=0)[BF16_ROWS:]
        return _silu(y)

    def l2norm(t):
        return t * lax.rsqrt(jnp.sum(t * t, axis=-1, keepdims=True) + 1e-6)

    heads = range(GDN_HEADS)
    sls = [slice(h * LANES, (h + 1) * LANES) for h in heads]
    dotf = functools.partial(jnp.dot, preferred_element_type=F32)

    a_mats, rhs = [], []
    for h in heads:
        sl = sls[h]
        q = l2norm(conv_silu(q_ref, hq_ref, 0, sl)) * (GDN_HEAD_DIM ** -0.5)
        k = l2norm(conv_silu(k_ref, hk_ref, 1, sl))
        v = conv_silu(v_ref, hv_ref, 2, sl)
        beta = beta_all[:, h:h + 1]
        gc = gc_all[:, GDN_HEADS + h:GDN_HEADS + h + 1]
        gcr = gct_all[GDN_HEADS + h:GDN_HEADS + h + 1, :]
        kb = k * beta
        kq = _dot_nt(jnp.concatenate([kb, q], axis=0), k)
        gcc = jnp.where(first_chunk, gc[:c], gc[c:])
        decay = jnp.where(ctril, jnp.exp(jnp.where(ctril, gcc - gcr, 0.0)), 0.0)
        a_mats.append(jnp.where(clower, compact(kq[:n]) * decay, 0.0))
        egc = jnp.exp(gc)
        rhs.append(jnp.concatenate([v * beta, kb * egc], axis=1))
        gl_e = gc[c - 1:c, :]
        gl_o = gc[n - 1:n, :]
        kd = k * jnp.exp(jnp.where(rowc < c, gl_e, gl_o) - gc)
        qd_ref[:, sl] = (q * egc).astype(qd_ref.dtype)
        ai_ref[:, sl] = (compact(kq[n:]) * decay).astype(ai_ref.dtype)
        kdt_ref[:, sl] = kd.T.astype(kdt_ref.dtype)
        gl_ref[0, 0:1, sl] = jnp.broadcast_to(jnp.exp(gl_e), (1, LANES))
        gl_ref[0, 1:2, sl] = jnp.broadcast_to(jnp.exp(gl_o), (1, LANES))

    def dot3(a_split, b_split):
        (ah, al), (bh, bl) = a_split, b_split
        return dotf(jnp.concatenate([ah, al, ah], axis=1), jnp.concatenate([bh, bh, bl], axis=0))

    def expand_split(s):
        return tuple(expand(t) for t in s)

    base_blk = 8
    diag = jnp.logical_and(clower, cblk < base_blk)
    zero_bf = jnp.zeros((c, n), BF16)
    a_splits = [_split(a) for a in a_mats]
    d_splits = [tuple(jnp.where(diag, t, zero_bf) for t in s) for s in a_splits]
    d2s = [dot3(d, expand_split(d)) for d in d_splits]
    xs = []
    for h in heads:
        p = ceye - jnp.where(diag, a_mats[h], 0.0)
        d2 = _split(d2s[h])
        ps_ = _split(p)
        m = dot3(tuple(jnp.concatenate([s, t], axis=0) for s, t in zip(d2, ps_)), expand_split(d2))
        d2s[h] = m[:c]
        xs.append(p + m[c:])
    for h in heads:
        xs[h] = xs[h] + dot3(_split(xs[h]), expand_split(_split(d2s[h])))
    b = base_blk
    while b < c:
        off = jnp.logical_and(clower, jnp.logical_and(cblk >= b, cblk < 2 * b))
        o_splits = [tuple(jnp.where(off, t, zero_bf) for t in s) for s in a_splits]
        x_splits = [_split(x) for x in xs]
        ys = [dot3(o_splits[h], expand_split(x_splits[h])) for h in heads]
        for h in heads:
            xs[h] = xs[h] - dot3(x_splits[h], expand_split(_split(ys[h])))
        b *= 2
    zero_rhs = jnp.zeros((c, 2 * LANES), BF16)

    def chunk_diag(t):
        return jnp.concatenate([jnp.concatenate([t[:c], zero_rhs], axis=1),
                                jnp.concatenate([zero_rhs, t[c:]], axis=1)], axis=0)

    for h in heads:
        sol = dot3(_split(xs[h]), tuple(chunk_diag(t) for t in _split(rhs[h])))
        for ci in range(2):
            rs = slice(ci * c, (ci + 1) * c)
            u_ref[rs, sls[h]] = sol[:, (2 * ci) * LANES:(2 * ci + 1) * LANES]
            w_ref[rs, sls[h]] = sol[:, (2 * ci + 1) * LANES:(2 * ci + 2) * LANES].astype(w_ref.dtype)


def _gdn_prepass(proj, ba, conv_w, alog_row, dtb_row, seq):
    t = proj.shape[0]
    gw = GDN_HEADS * GDN_HEAD_DIM
    n = GDN_PAIR
    ppb = seq // n
    hb = n // BF16_ROWS

    def cur(cb):
        return pl.BlockSpec((n, gw), lambda i: (i, cb))

    def halo(cb):
        return pl.BlockSpec((BF16_ROWS, gw), lambda i: (jnp.maximum(i * hb - 1, 0), cb))

    row_spec = pl.BlockSpec((n, gw), lambda i: (i, 0))
    return pl.pallas_call(
        functools.partial(_gdn_pre_kernel, pairs_per_batch=ppb),
        grid=(t // n,),
        in_specs=[cur(0), cur(1), cur(2), halo(0), halo(1), halo(2),
                  pl.BlockSpec((n, LANES), lambda i: (i, 0)),
                  pl.BlockSpec((GDN_CONV, 3 * gw), lambda i: (0, 0)),
                  pl.BlockSpec((1, LANES), lambda i: (0, 0)),
                  pl.BlockSpec((1, LANES), lambda i: (0, 0))],
        out_specs=[row_spec, row_spec, row_spec,
                   pl.BlockSpec((GDN_CHUNK, gw), lambda i: (i, 0)), row_spec,
                   pl.BlockSpec((1, 2, gw), lambda i: (i, 0, 0))],
        out_shape=[jax.ShapeDtypeStruct((t, gw), F32),
                   jax.ShapeDtypeStruct((t, gw), BF16),
                   jax.ShapeDtypeStruct((t, gw), BF16),
                   jax.ShapeDtypeStruct((t // 2, gw), BF16),
                   jax.ShapeDtypeStruct((t, gw), BF16),
                   jax.ShapeDtypeStruct((t // n, 2, gw), F32)],
        compiler_params=_params("parallel"),
        name="gdn_prepass",
    )(proj, proj, proj, proj, proj, proj, ba, conv_w, alog_row, dtb_row)


def _gdn_scan_kernel(u_ref, w_ref, qd_ref, ai_ref, kdt_ref, gl_ref, z_ref, nw_ref, o_ref, st_ref):
    c = GDN_CHUNK

    @pl.when(pl.program_id(1) == 0)
    def _():
        st_ref[...] = jnp.zeros_like(st_ref)

    zeros = jnp.zeros((c, LANES), F32)
    nw = nw_ref[...]
    heads = range(GDN_HEADS)
    sls = [slice(h * LANES, (h + 1) * LANES) for h in heads]
    states = [st_ref[h] for h in heads]
    for ci in range(2 * gl_ref.shape[0]):
        pair, half = divmod(ci, 2)
        rs = slice(ci * c, (ci + 1) * c)
        ps = slice(pair * GDN_PAIR, (pair + 1) * GDN_PAIR)
        sbs = [s.astype(BF16) for s in states]
        ws = [_dot(w_ref[rs, sls[h]], sbs[h]) for h in heads]
        qs = [_dot(qd_ref[rs, sls[h]], sbs[h]) for h in heads]
        vpads = []
        for h in heads:
            vnew = u_ref[rs, sls[h]] - ws[h]
            vpads.append(jnp.concatenate([vnew, zeros] if half == 0 else [zeros, vnew], axis=0).astype(BF16))
        outs = [qs[h] + _dot(ai_ref[pair * c:(pair + 1) * c, sls[h]], vpads[h]) for h in heads]
        states = [states[h] * gl_ref[pair, half:half + 1, sls[h]] + _dot(kdt_ref[ps, sls[h]], vpads[h])
                  for h in heads]
        for h in heads:
            o = outs[h]
            z = z_ref[rs, sls[h]].astype(F32)
            o = o * lax.rsqrt(jnp.mean(o * o, axis=-1, keepdims=True) + 1e-6) * nw * _silu(z)
            o_ref[rs, sls[h]] = o.astype(o_ref.dtype)
    for h in heads:
        st_ref[h] = states[h]


def _gdn_scan(u, w, qd, ai, kdt, gl, proj, z_col, norm_w_row, batch, seq, pairs):
    t = u.shape[0]
    gw = GDN_HEADS * GDN_HEAD_DIM
    n = pairs * GDN_PAIR
    ppb = seq // n
    row_spec = pl.BlockSpec((n, gw), lambda b, i: (b * ppb + i, 0))
    return pl.pallas_call(
        _gdn_scan_kernel,
        grid=(batch, ppb),
        in_specs=[row_spec, row_spec, row_spec,
                  pl.BlockSpec((n // 2, gw), lambda b, i: (b * ppb + i, 0)), row_spec,
                  pl.BlockSpec((pairs, 2, gw), lambda b, i: (b * ppb + i, 0, 0)),
                  pl.BlockSpec((n, gw), lambda b, i: (b * ppb + i, z_col)),
                  pl.BlockSpec((1, LANES), lambda b, i: (0, 0))],
        out_specs=row_spec,
        out_shape=jax.ShapeDtypeStruct((t, gw), BF16),
        scratch_shapes=[pltpu.VMEM((GDN_HEADS, GDN_HEAD_DIM, GDN_HEAD_DIM), F32)],
        compiler_params=_params("arbitrary", "arbitrary"),
        name="gdn_scan",
    )(u, w, qd, ai, kdt, gl, proj, norm_w_row)


def _merge_kernel(x_ref, a_ref, g_ref, wgs_ref, wgg_ref, wbs_ref, wbg_ref, o_ref):
    x = x_ref[...]
    gate_s = _sigmoid(jnp.dot(x, wgs_ref[...], preferred_element_type=F32))
    gate_g = _sigmoid(jnp.dot(x, wgg_ref[...], preferred_element_type=F32))
    y_s = jnp.dot(a_ref[...], wbs_ref[...], preferred_element_type=F32)
    y_g = jnp.dot(g_ref[...], wbg_ref[...], preferred_element_type=F32)
    o_ref[...] = (gate_s * y_s + gate_g * y_g).astype(o_ref.dtype)


def _merge(xb, attn, gdn, wgs, wgg, wbs, wbg, tm, tn):
    t, d = xb.shape
    ws = attn.shape[1]
    wg = gdn.shape[1]

    def wspec(k):
        return pl.BlockSpec((k, tn), lambda i, j: (0, j))

    return pl.pallas_call(
        _merge_kernel,
        grid=(t // tm, d // tn),
        in_specs=[pl.BlockSpec((tm, d), lambda i, j: (i, 0)),
                  pl.BlockSpec((tm, ws), lambda i, j: (i, 0)),
                  pl.BlockSpec((tm, wg), lambda i, j: (i, 0)),
                  wspec(d), wspec(d), wspec(ws), wspec(wg)],
        out_specs=pl.BlockSpec((tm, tn), lambda i, j: (i, j)),
        out_shape=jax.ShapeDtypeStruct((t, d), BF16),
        compiler_params=_params("parallel", "arbitrary"),
        name="branch_merge",
    )(xb, attn, gdn, wgs, wgg, wbs, wbg)


def _mixo_ln_kernel(m_ref, x_ref, w_ref, g_ref, b_ref, o_ref, *, alpha):
    y = alpha * x_ref[...] + jnp.dot(m_ref[...], w_ref[...], preferred_element_type=F32)
    o_ref[...] = _layer_norm(y, g_ref[...], b_ref[...])


def _mixo_ln(mixed, x, w, g, b, alpha, tm):
    t, d = x.shape
    vec = pl.BlockSpec((1, d), lambda i: (0, 0))
    return pl.pallas_call(
        functools.partial(_mixo_ln_kernel, alpha=alpha),
        grid=(t // tm,),
        in_specs=[pl.BlockSpec((tm, d), lambda i: (i, 0)),
                  pl.BlockSpec((tm, d), lambda i: (i, 0)),
                  pl.BlockSpec((d, d), lambda i: (0, 0)),
                  vec, vec],
        out_specs=pl.BlockSpec((tm, d), lambda i: (i, 0)),
        out_shape=jax.ShapeDtypeStruct((t, d), F32),
        compiler_params=_params("parallel"),
        name="mixo_layernorm",
    )(mixed, x, w, g, b)


def _mem_kernel(x_ref, wq_ref, kv_ref, wo_ref, g_ref, b_ref, o_ref, *, alpha):
    x = x_ref[...]
    q = jnp.dot(x.astype(BF16), wq_ref[...], preferred_element_type=F32)
    mw = MEM_HEADS * MEM_HEAD_DIM
    outs = []
    qb = q.astype(BF16)
    scores = [_dot_nt(qb[:, h * MEM_HEAD_DIM:(h + 1) * MEM_HEAD_DIM],
                      kv_ref[:, h * MEM_HEAD_DIM:(h + 1) * MEM_HEAD_DIM]) for h in range(MEM_HEADS)]
    es, inv = [], []
    for h in range(MEM_HEADS):
        s = scores[h] * (MEM_HEAD_DIM ** -0.5)
        m = jnp.max(s, axis=-1, keepdims=True)
        e = jnp.exp(s - m)
        inv.append(1.0 / jnp.sum(e, axis=-1, keepdims=True))
        es.append(e.astype(BF16))
    for h in range(MEM_HEADS):
        vh = kv_ref[:, mw + h * MEM_HEAD_DIM:mw + (h + 1) * MEM_HEAD_DIM]
        outs.append(_dot(es[h], vh) * inv[h])
    o = jnp.concatenate(outs, axis=1)
    y = alpha * x + _dot(o, wo_ref[...])
    o_ref[...] = _layer_norm(y, g_ref[...], b_ref[...])


def _mem_attention_ln(x, wq, kv, wo, g, b, alpha, batch, seq, mem_len, tm):
    t, d = x.shape
    mw = MEM_HEADS * MEM_HEAD_DIM
    tpb = seq // tm
    vec = pl.BlockSpec((1, d), lambda bi, i: (0, 0))
    return pl.pallas_call(
        functools.partial(_mem_kernel, alpha=alpha),
        grid=(batch, tpb),
        in_specs=[pl.BlockSpec((tm, d), lambda bi, i: (bi * tpb + i, 0)),
                  pl.BlockSpec((d, mw), lambda bi, i: (0, 0)),
                  pl.BlockSpec((mem_len, 2 * mw), lambda bi, i: (bi, 0)),
                  pl.BlockSpec((mw, d), lambda bi, i: (0, 0)),
                  vec, vec],
        out_specs=pl.BlockSpec((tm, d), lambda bi, i: (bi * tpb + i, 0)),
        out_shape=jax.ShapeDtypeStruct((t, d), F32),
        compiler_params=_params("parallel", "arbitrary"),
        name="mem_attention_layernorm",
    )(x, wq, kv, wo, g, b)


def _ffn_kernel(x_ref, halo_ref, wg_ref, wu_ref, cg_ref, cu_ref, bg_ref, bu_ref, wd_ref, g_ref, b_ref,
                o_ref, xb_ref, acc_ref, *, alpha, tiles_per_batch):
    j = pl.program_id(1)
    hr = BF16_ROWS

    @pl.when(j == 0)
    def _():
        first = (pl.program_id(0) % tiles_per_batch) == 0
        xb_ref[:hr, :] = jnp.where(first, 0.0, halo_ref[...]).astype(BF16)
        xb_ref[hr:, :] = x_ref[...].astype(BF16)
        acc_ref[...] = jnp.zeros_like(acc_ref)

    xb = xb_ref[...]

    def conv(w_ref, cw_ref, bias_ref):
        h = jnp.dot(xb, w_ref[...], preferred_element_type=F32)
        cw = cw_ref[...]
        y = cw[FFN_CONV - 1:FFN_CONV] * h[hr:] + bias_ref[...]
        for k in range(1, FFN_CONV):
            y = y + cw[FFN_CONV - 1 - k:FFN_CONV - k] * pltpu.roll(h, k, axis=0)[hr:]
        return y

    act = _silu(conv(wg_ref, cg_ref, bg_ref)) * conv(wu_ref, cu_ref, bu_ref)
    acc_ref[...] += _dot(act, wd_ref[...])

    @pl.when(j == pl.num_programs(1) - 1)
    def _():
        y = alpha * x_ref[...] + acc_ref[...]
        o_ref[...] = _layer_norm(y, g_ref[...], b_ref[...])


def _ffn_ln(x, wg, wu, cg, cu, bg, bu, wd, g, b, alpha, seq, tm, tn):
    t, d = x.shape
    ff = wg.shape[1]
    tpb = seq // tm
    hb = tm // BF16_ROWS
    vec = pl.BlockSpec((1, d), lambda i, j: (0, 0))
    return pl.pallas_call(
        functools.partial(_ffn_kernel, alpha=alpha, tiles_per_batch=tpb),
        grid=(t // tm, ff // tn),
        in_specs=[pl.BlockSpec((tm, d), lambda i, j: (i, 0)),
                  pl.BlockSpec((BF16_ROWS, d), lambda i, j: (jnp.maximum(i * hb - 1, 0), 0)),
                  pl.BlockSpec((d, tn), lambda i, j: (0, j)),
                  pl.BlockSpec((d, tn), lambda i, j: (0, j)),
                  pl.BlockSpec((FFN_CONV, tn), lambda i, j: (0, j)),
                  pl.BlockSpec((FFN_CONV, tn), lambda i, j: (0, j)),
                  pl.BlockSpec((1, tn), lambda i, j: (0, j)),
                  pl.BlockSpec((1, tn), lambda i, j: (0, j)),
                  pl.BlockSpec((tn, d), lambda i, j: (j, 0)),
                  vec, vec],
        out_specs=pl.BlockSpec((tm, d), lambda i, j: (i, 0)),
        out_shape=jax.ShapeDtypeStruct((t, d), F32),
        scratch_shapes=[pltpu.VMEM((BF16_ROWS + tm, d), BF16),
                        pltpu.VMEM((tm, d), F32)],
        compiler_params=_params("parallel", "arbitrary"),
        name="ffn_layernorm",
    )(x, x, wg, wu, cg, cu, bg, bu, wd, g, b)


def _pad_cols(a, n):
    return jnp.pad(a, ((0, 0), (0, n - a.shape[1])))


def _layer(x2d, mem2d, batch, seq, mem_len, p, bias_tab, alpha, tiles):
    d = x2d.shape[1]
    sq = SWA_HEADS * SWA_HEAD_DIM
    skv = SWA_KV_HEADS * SWA_HEAD_DIM
    gw = GDN_HEADS * GDN_HEAD_DIM
    w_in = p["w_in"]
    o_gdn = sq + 2 * skv
    o_ba = o_gdn + 4 * gw
    o_gate = o_ba + 2 * GDN_HEADS
    w_proj = jnp.concatenate([w_in[:, o_gdn:o_ba], w_in[:, :sq] * (SWA_HEAD_DIM ** -0.5),
                              w_in[:, sq:o_gdn]], axis=1).astype(BF16)
    w_ba = _pad_cols(w_in[:, o_ba:o_gate], LANES).astype(BF16)
    w_gs = w_in[:, o_gate:o_gate + d].astype(BF16)
    w_gg = w_in[:, o_gate + d:o_gate + 2 * d].astype(BF16)

    proj, xb, ba = _proj(x2d, w_proj, w_ba, tiles["proj_tm"], tiles["proj_tn"])

    attn = _swa_attention(proj, (4 * gw) // sq, (4 * gw + sq) // (2 * skv), p["swa_sinks"], bias_tab, seq,
                          tiles["swa_blocks"])

    zpad = jnp.zeros((GDN_HEADS,), F32)
    alog_row = _pad_cols(jnp.concatenate([zpad, p["gdn_a_log"]])[None, :], LANES)
    dtb_row = _pad_cols(jnp.concatenate([zpad, p["gdn_dt_bias"]])[None, :], LANES)
    u, w, qd, ai, kdt, gl = _gdn_prepass(proj, ba, p["gdn_conv_w"], alog_row, dtb_row, seq)
    gdn = _gdn_scan(u, w, qd, ai, kdt, gl, proj, 3, p["gdn_norm_w"][None, :], batch, seq,
                    tiles["scan_pairs"])

    mixed = _merge(xb, attn, gdn, w_gs, w_gg, p["w_br_swa"].astype(BF16), p["w_br_gdn"].astype(BF16),
                   tiles["merge_tm"], tiles["merge_tn"])
    x1 = _mixo_ln(mixed, x2d, p["w_mix_o"].astype(BF16), p["ln1_g"][None, :], p["ln1_b"][None, :],
                  alpha, tiles["ln_tm"])

    kv = _matmul(mem2d.astype(BF16), p["w_mem_kv"].astype(BF16), mem_len, tiles["kv_tn"], BF16)
    x2 = _mem_attention_ln(x1, p["w_mem_q"].astype(BF16), kv, p["w_mem_o"].astype(BF16),
                           p["ln2_g"][None, :], p["ln2_b"][None, :], alpha, batch, seq, mem_len,
                           tiles["mem_tm"])

    dff = p["w_down"].shape[0]
    tn = tiles["ffn_tn"]
    ffp = -(-dff // tn) * tn
    w_up = p["w_up"]
    cw = p["ffn_conv_w"]
    cb = p["ffn_conv_b"][None, :]
    x3 = _ffn_ln(x2,
                 _pad_cols(w_up[:, :dff], ffp).astype(BF16), _pad_cols(w_up[:, dff:], ffp).astype(BF16),
                 _pad_cols(cw[:, :dff], ffp), _pad_cols(cw[:, dff:], ffp),
                 _pad_cols(cb[:, :dff], ffp), _pad_cols(cb[:, dff:], ffp),
                 jnp.pad(p["w_down"], ((0, ffp - dff), (0, 0))).astype(BF16),
                 p["ln3_g"][None, :], p["ln3_b"][None, :], alpha, seq, tiles["ffn_tm"], tn)
    return x3


_TILES = dict(proj_tm=1024, proj_tn=1792, merge_tm=1024, merge_tn=1024, ln_tm=512, kv_tn=512,
              mem_tm=512, ffn_tm=512, ffn_tn=512, swa_blocks=2, scan_pairs=2)

_PER_LAYER = ("w_in", "swa_sinks", "gdn_conv_w", "gdn_a_log", "gdn_dt_bias", "gdn_norm_w", "w_br_swa",
              "w_br_gdn", "w_mix_o", "ln1_g", "ln1_b", "w_mem_q", "w_mem_kv", "w_mem_o", "ln2_g", "ln2_b",
              "w_up", "ffn_conv_w", "ffn_conv_b", "w_down", "ln3_g", "ln3_b")


def _forward(x, mem, rel_bias, weights, tiles):
    batch, seq, d = x.shape
    mem_len = mem.shape[1]
    depth = weights["w_in"].shape[0]
    alpha = (2 * depth) ** 0.25
    bias_tab = _swa_bias_table(rel_bias)
    x2d = x.reshape(batch * seq, d)
    mem2d = mem.reshape(batch * mem_len, d)
    for l in range(depth):
        p = {name: weights[name][l] for name in _PER_LAYER}
        x2d = _layer(x2d, mem2d, batch, seq, mem_len, p, bias_tab, alpha, tiles)
    return x2d.reshape(batch, seq, d)


def kernel(x, mem, w_in, rel_bias, swa_sinks, gdn_conv_w, gdn_a_log, gdn_dt_bias, gdn_norm_w, w_br_swa, w_br_gdn, w_mix_o, ln1_g, ln1_b, w_mem_q, w_mem_kv, w_mem_o, ln2_g, ln2_b, w_up, ffn_conv_w, ffn_conv_b, w_down, ln3_g, ln3_b):
    weights = dict(w_in=w_in, swa_sinks=swa_sinks, gdn_conv_w=gdn_conv_w, gdn_a_log=gdn_a_log,
                   gdn_dt_bias=gdn_dt_bias, gdn_norm_w=gdn_norm_w, w_br_swa=w_br_swa, w_br_gdn=w_br_gdn,
                   w_mix_o=w_mix_o, ln1_g=ln1_g, ln1_b=ln1_b, w_mem_q=w_mem_q, w_mem_kv=w_mem_kv,
                   w_mem_o=w_mem_o, ln2_g=ln2_g, ln2_b=ln2_b, w_up=w_up, ffn_conv_w=ffn_conv_w,
                   ffn_conv_b=ffn_conv_b, w_down=w_down, ln3_g=ln3_g, ln3_b=ln3_b)
    return _forward(x, mem, rel_bias, weights, _TILES)
```

```python
import functools
import math

import jax
import jax.numpy as jnp
from jax import lax
from jax.experimental import pallas as pl
from jax.experimental.pallas import tpu as pltpu

F32 = jnp.float32
BF16 = jnp.bfloat16

LANES = 128
BF16_ROWS = 16
VMEM_LIMIT = 56 * 1024 * 1024

SWA_HEADS = 16
SWA_KV_HEADS = 2
SWA_HEAD_DIM = 64
SWA_BLOCK = 128
REL_BUCKETS = 32
REL_MAX_DIST = 128
GDN_HEADS = 8
GDN_HEAD_DIM = 128
GDN_CONV = 4
GDN_CHUNK = 64
GDN_PAIR = 2 * GDN_CHUNK
MEM_HEADS = 4
MEM_HEAD_DIM = 128
FFN_CONV = 3
NORM_EPS = 1e-5
NEG_INF = -1e30


def _params(*sem):
    return pltpu.CompilerParams(dimension_semantics=sem, vmem_limit_bytes=VMEM_LIMIT)


def _dot(a, b):
    return jnp.dot(a.astype(BF16), b.astype(BF16), preferred_element_type=F32)


def _dot_nt(a, b):
    return lax.dot_general(a.astype(BF16), b.astype(BF16), (((1,), (1,)), ((), ())),
                           preferred_element_type=F32)


def _split(a):
    hi = a.astype(BF16)
    lo = (a - hi.astype(F32)).astype(BF16)
    return hi, lo


def _dot3(a, b):
    ah, al = _split(a)
    bh, bl = _split(b)
    d = functools.partial(jnp.dot, preferred_element_type=F32)
    return d(ah, bh) + (d(al, bh) + d(ah, bl))


def _sigmoid(x):
    return 1.0 / (1.0 + jnp.exp(-x))


def _silu(x):
    return x * _sigmoid(x)


def _layer_norm(y, g, b):
    mu = jnp.mean(y, axis=-1, keepdims=True)
    yc = y - mu
    var = jnp.mean(yc * yc, axis=-1, keepdims=True)
    return yc * lax.rsqrt(var + NORM_EPS) * g + b


def _mm_kernel(x_ref, w_ref, o_ref):
    o_ref[...] = jnp.dot(x_ref[...], w_ref[...], preferred_element_type=F32).astype(o_ref.dtype)


def _matmul(x, w, tm, tn, out_dtype):
    m, k = x.shape
    n = w.shape[1]
    assert m % tm == 0 and n % tn == 0
    return pl.pallas_call(
        _mm_kernel,
        grid=(m // tm, n // tn),
        in_specs=[pl.BlockSpec((tm, k), lambda i, j: (i, 0)),
                  pl.BlockSpec((k, tn), lambda i, j: (0, j))],
        out_specs=pl.BlockSpec((tm, tn), lambda i, j: (i, j)),
        out_shape=jax.ShapeDtypeStruct((m, n), out_dtype),
        compiler_params=_params("parallel", "arbitrary"),
        name="matmul",
    )(x, w)


def _proj_kernel(x_ref, w_ref, ws_ref, o_ref, xb_ref, os_ref):
    @pl.when(pl.program_id(1) == 0)
    def _():
        xb_ref[...] = x_ref[...].astype(xb_ref.dtype)
        os_ref[...] = jnp.dot(xb_ref[...], ws_ref[...], preferred_element_type=F32)

    o_ref[...] = jnp.dot(xb_ref[...], w_ref[...], preferred_element_type=F32).astype(o_ref.dtype)


def _proj(x, w, w_small, tm, tn):
    m, k = x.shape
    n = w.shape[1]
    ns = w_small.shape[1]
    assert m % tm == 0 and n % tn == 0
    return pl.pallas_call(
        _proj_kernel,
        grid=(m // tm, n // tn),
        in_specs=[pl.BlockSpec((tm, k), lambda i, j: (i, 0)),
                  pl.BlockSpec((k, tn), lambda i, j: (0, j)),
                  pl.BlockSpec((k, ns), lambda i, j: (0, 0))],
        out_specs=[pl.BlockSpec((tm, tn), lambda i, j: (i, j)),
                   pl.BlockSpec((tm, k), lambda i, j: (i, 0)),
                   pl.BlockSpec((tm, ns), lambda i, j: (i, 0))],
        out_shape=[jax.ShapeDtypeStruct((m, n), BF16), jax.ShapeDtypeStruct((m, k), BF16),
                   jax.ShapeDtypeStruct((m, ns), F32)],
        compiler_params=_params("parallel", "arbitrary"),
        name="input_proj",
    )(x, w, w_small)


def _t5_causal_bucket(dist):
    max_exact = REL_BUCKETS // 2
    d = jnp.maximum(dist, 1).astype(F32)
    large = max_exact + (jnp.log(d / max_exact) / math.log(REL_MAX_DIST / max_exact)
                         * (REL_BUCKETS - max_exact)).astype(jnp.int32)
    large = jnp.minimum(large, REL_BUCKETS - 1)
    return jnp.where(dist < max_exact, dist, large)


def _swa_bias_table(rel_bias):
    nq, nk = SWA_BLOCK, 2 * SWA_BLOCK
    dist = nk - 1 - jnp.arange(nq + nk - 1)
    per_dist = rel_bias.astype(F32)[_t5_causal_bucket(jnp.maximum(dist, 0))]
    per_dist = jnp.where(((dist >= 0) & (dist < SWA_BLOCK))[:, None], per_dist, NEG_INF)
    u = jnp.pad(per_dist.T, ((0, 0), (0, 1)))
    skew = jnp.tile(u, (1, nq))[:, :nq * (nq + nk - 1)].reshape(-1, nq, nq + nk - 1)
    table = skew[:, :, nq - 1:]
    no_prev = jnp.where(jnp.arange(nk) < nq, NEG_INF, table)
    return jnp.stack([table, no_prev])


def _swa_kernel(sink_ref, q_ref, kv_ref, pkv_ref, bias0_ref, bias_ref, o_ref, *, blocks):
    half = SWA_HEAD_DIM
    kv = jnp.concatenate([pkv_ref[...], kv_ref[...]], axis=0)
    k = kv[:, :LANES]
    v = kv[:, LANES:]
    lo = lax.broadcasted_iota(jnp.int32, k.shape, 1) < half
    zero = jnp.zeros_like(k)

    def swap(t):
        return jnp.concatenate([t[:, half:], t[:, :half]], axis=1)

    def padded(t):
        tr = swap(t)
        return ((jnp.where(lo, t, zero), jnp.where(lo, zero, tr)),
                (jnp.where(lo, tr, zero), jnp.where(lo, zero, t)))

    kpad = padded(k)
    vpad = padded(v)
    out_lo = lax.broadcasted_iota(jnp.int32, (SWA_BLOCK, LANES), 1) < half
    group = SWA_HEADS // SWA_KV_HEADS
    for blk in range(blocks):
        rows = slice(blk * SWA_BLOCK, (blk + 1) * SWA_BLOCK)
        band = slice(blk * SWA_BLOCK, (blk + 2) * SWA_BLOCK)
        table_ref = bias0_ref if blk == 0 else bias_ref
        scores = [_dot_nt(q_ref[rows, (head // 2) * LANES:(head // 2 + 1) * LANES],
                          kpad[head // group][head % 2][band]) for head in range(SWA_HEADS)]
        es, inv = [], []
        for head in range(SWA_HEADS):
            s = scores[head] + table_ref[head]
            sink = sink_ref[head]
            m = jnp.maximum(jnp.max(s, axis=-1, keepdims=True), sink)
            e = jnp.exp(s - m)
            inv.append(1.0 / (jnp.sum(e, axis=-1, keepdims=True) + jnp.exp(sink - m)))
            es.append(e.astype(BF16))
        for p in range(SWA_HEADS // 2):
            acc = (_dot(es[2 * p], vpad[(2 * p) // group][0][band])
                   + _dot(es[2 * p + 1], vpad[(2 * p + 1) // group][1][band]))
            o_ref[rows, p * LANES:(p + 1) * LANES] = (
                acc * jnp.where(out_lo, inv[2 * p], inv[2 * p + 1])).astype(o_ref.dtype)


def _swa_attention(proj, q_col, kv_col, sinks, bias_tabs, seq, blocks):
    t = proj.shape[0]
    qw = SWA_HEADS * SWA_HEAD_DIM
    kvw = 2 * SWA_KV_HEADS * SWA_HEAD_DIM
    tq = blocks * SWA_BLOCK
    spb = seq // tq
    table = (None, SWA_HEADS, SWA_BLOCK, 2 * SWA_BLOCK)
    return pl.pallas_call(
        functools.partial(_swa_kernel, blocks=blocks),
        grid=(t // tq,),
        in_specs=[pl.BlockSpec(memory_space=pltpu.SMEM),
                  pl.BlockSpec((tq, qw), lambda i: (i, q_col)),
                  pl.BlockSpec((tq, kvw), lambda i: (i, kv_col)),
                  pl.BlockSpec((SWA_BLOCK, kvw), lambda i: (jnp.maximum(i * blocks - 1, 0), kv_col)),
                  pl.BlockSpec(table, lambda i: (jnp.where(i % spb == 0, 1, 0), 0, 0, 0)),
                  pl.BlockSpec(table, lambda i: (0, 0, 0, 0))],
        out_specs=pl.BlockSpec((tq, qw), lambda i: (i, 0)),
        out_shape=jax.ShapeDtypeStruct((t, qw), BF16),
        compiler_params=_params("parallel"),
        name="swa_attention",
    )(sinks, proj, proj, proj, bias_tabs, bias_tabs)


def _gdn_pre_kernel(q_ref, k_ref, v_ref, hq_ref, hk_ref, hv_ref, ba_ref, cw_ref, alog_ref, dtb_ref,
                    u_ref, w_ref, qd_ref, ai_ref, kdt_ref, gl_ref, *, pairs_per_batch):
    n = GDN_PAIR
    c = GDN_CHUNK
    first = (pl.program_id(0) % pairs_per_batch) == 0
    row = lax.broadcasted_iota(jnp.int32, (n, n), 0)
    col = lax.broadcasted_iota(jnp.int32, (n, n), 1)
    same = (row >= c) == (col >= c)
    tril = jnp.logical_and(same, row >= col)
    rowc = lax.broadcasted_iota(jnp.int32, (n, 1), 0)
    crow = lax.broadcasted_iota(jnp.int32, (c, n), 0)
    clane = lax.broadcasted_iota(jnp.int32, (c, n), 1)
    first_chunk = clane < c
    ccol = clane & (c - 1)
    ctril = crow >= ccol
    clower = crow > ccol
    cblk = crow ^ ccol
    ceye = jnp.where(crow == ccol, 1.0, 0.0).astype(F32)

    def compact(m):
        return jnp.where(first_chunk, m[:c], m[c:])

    def expand(m):
        z = jnp.zeros_like(m)
        return jnp.concatenate([jnp.where(first_chunk, m, z), jnp.where(first_chunk, z, m)], axis=0)

    ba = ba_ref[...]
    beta_all = _sigmoid(ba)
    x = ba + dtb_ref[...]
    softplus = jnp.maximum(x, 0.0) + jnp.log1p(jnp.exp(-jnp.abs(x)))
    g_all = -jnp.exp(alog_ref[...]) * softplus
    gc_all = _dot3(jnp.where(tril, 1.0, 0.0).astype(F32), g_all)
    gct_all = gc_all.T

    def conv_silu(cur_ref, halo_ref, which, sl):
        cur = cur_ref[:, sl].astype(F32)
        halo = jnp.where(first, 0.0, halo_ref[:, sl].astype(F32))
        ext = jnp.concatenate([halo, cur], axis=0)
        base = which * GDN_HEADS * GDN_HEAD_DIM
        cw = cw_ref[:, base + sl.start:base + sl.stop]
        y = cw[GDN_CONV - 1:GDN_CONV] * cur
        for j in range(1, GDN_CONV):
            y = y + cw[GDN_CONV - 1 - j:GDN_CONV - j] * pltpu.roll(ext, j, axis=0)[BF16_ROWS:]
        return _silu(y)

    def l2norm(t):
        return t * lax.rsqrt(jnp.sum(t * t, axis=-1, keepdims=True) + 1e-6)

    heads = range(GDN_HEADS)
    sls = [slice(h * LANES, (h + 1) * LANES) for h in heads]
    dotf = functools.partial(jnp.dot, preferred_element_type=F32)

    a_mats, rhs = [], []
    for h in heads:
        sl = sls[h]
        q = l2norm(conv_silu(q_ref, hq_ref, 0, sl)) * (GDN_HEAD_DIM ** -0.5)
        k = l2norm(conv_silu(k_ref, hk_ref, 1, sl))
        v = conv_silu(v_ref, hv_ref, 2, sl)
        beta = beta_all[:, h:h + 1]
        gc = gc_all[:, GDN_HEADS + h:GDN_HEADS + h + 1]
        gcr = gct_all[GDN_HEADS + h:GDN_HEADS + h + 1, :]
        kb = k * beta
        kq = _dot_nt(jnp.concatenate([kb, q], axis=0), k)
        gcc = jnp.where(first_chunk, gc[:c], gc[c:])
        decay = jnp.where(ctril, jnp.exp(jnp.where(ctril, gcc - gcr, 0.0)), 0.0)
        a_mats.append(jnp.where(clower, compact(kq[:n]) * decay, 0.0))
        egc = jnp.exp(gc)
        rhs.append(jnp.concatenate([v * beta, kb * egc], axis=1))
        gl_e = gc[c - 1:c, :]
        gl_o = gc[n - 1:n, :]
        kd = k * jnp.exp(jnp.where(rowc < c, gl_e, gl_o) - gc)
        qd_ref[:, sl] = (q * egc).astype(qd_ref.dtype)
        ai_ref[:, sl] = (compact(kq[n:]) * decay).astype(ai_ref.dtype)
        kdt_ref[:, sl] = kd.T.astype(kdt_ref.dtype)
        gl_ref[0, 0:1, sl] = jnp.broadcast_to(jnp.exp(gl_e), (1, LANES))
        gl_ref[0, 1:2, sl] = jnp.broadcast_to(jnp.exp(gl_o), (1, LANES))

    def dot3(a_split, b_split):
        (ah, al), (bh, bl) = a_split, b_split
        return dotf(jnp.concatenate([ah, al, ah], axis=1), jnp.concatenate([bh, bh, bl], axis=0))

    def expand_split(s):
        return tuple(expand(t) for t in s)

    base_blk = 8
    diag = jnp.logical_and(clower, cblk < base_blk)
    zero_bf = jnp.zeros((c, n), BF16)
    a_splits = [_split(a) for a in a_mats]
    d_splits = [tuple(jnp.where(diag, t, zero_bf) for t in s) for s in a_splits]
    d2s = [dot3(d, expand_split(d)) for d in d_splits]
    xs = []
    for h in heads:
        p = ceye - jnp.where(diag, a_mats[h], 0.0)
        d2 = _split(d2s[h])
        ps_ = _split(p)
        m = dot3(tuple(jnp.concatenate([s, t], axis=0) for s, t in zip(d2, ps_)), expand_split(d2))
        d2s[h] = m[:c]
        xs.append(p + m[c:])
    for h in heads:
        xs[h] = xs[h] + dot3(_split(xs[h]), expand_split(_split(d2s[h])))
    b = base_blk
    while b < c:
        off = jnp.logical_and(clower, jnp.logical_and(cblk >= b, cblk < 2 * b))
        o_splits = [tuple(jnp.where(off, t, zero_bf) for t in s) for s in a_splits]
        x_splits = [_split(x) for x in xs]
        ys = [dot3(o_splits[h], expand_split(x_splits[h])) for h in heads]
        for h in heads:
            xs[h] = xs[h] - dot3(x_splits[h], expand_split(_split(ys[h])))
        b *= 2
    zero_rhs = jnp.zeros((c, 2 * LANES), BF16)

    def chunk_diag(t):
        return jnp.concatenate([jnp.concatenate([t[:c], zero_rhs], axis=1),
                                jnp.concatenate([zero_rhs, t[c:]], axis=1)], axis=0)

    for h in heads:
        sol = dot3(_split(xs[h]), tuple(chunk_diag(t) for t in _split(rhs[h])))
        for ci in range(2):
            rs = slice(ci * c, (ci + 1) * c)
            u_ref[rs, sls[h]] = sol[:, (2 * ci) * LANES:(2 * ci + 1) * LANES]
            w_ref[rs, sls[h]] = sol[:, (2 * ci + 1) * LANES:(2 * ci + 2) * LANES].astype(w_ref.dtype)


def _gdn_prepass(proj, ba, conv_w, alog_row, dtb_row, seq):
    t = proj.shape[0]
    gw = GDN_HEADS * GDN_HEAD_DIM
    n = GDN_PAIR
    ppb = seq // n
    hb = n // BF16_ROWS

    def cur(cb):
        return pl.BlockSpec((n, gw), lambda i: (i, cb))

    def halo(cb):
        return pl.BlockSpec((BF16_ROWS, gw), lambda i: (jnp.maximum(i * hb - 1, 0), cb))

    row_spec = pl.BlockSpec((n, gw), lambda i: (i, 0))
    return pl.pallas_call(
        functools.partial(_gdn_pre_kernel, pairs_per_batch=ppb),
        grid=(t // n,),
        in_specs=[cur(0), cur(1), cur(2), halo(0), halo(1), halo(2),
                  pl.BlockSpec((n, LANES), lambda i: (i, 0)),
                  pl.BlockSpec((GDN_CONV, 3 * gw), lambda i: (0, 0)),
                  pl.BlockSpec((1, LANES), lambda i: (0, 0)),
                  pl.BlockSpec((1, LANES), lambda i: (0, 0))],
        out_specs=[row_spec, row_spec, row_spec,
                   pl.BlockSpec((GDN_CHUNK, gw), lambda i: (i, 0)), row_spec,
                   pl.BlockSpec((1, 2, gw), lambda i: (i, 0, 0))],
        out_shape=[jax.ShapeDtypeStruct((t, gw), F32),
                   jax.ShapeDtypeStruct((t, gw), BF16),
                   jax.ShapeDtypeStruct((t, gw), BF16),
                   jax.ShapeDtypeStruct((t // 2, gw), BF16),
                   jax.ShapeDtypeStruct((t, gw), BF16),
                   jax.ShapeDtypeStruct((t // n, 2, gw), F32)],
        compiler_params=_params("parallel"),
        name="gdn_prepass",
    )(proj, proj, proj, proj, proj, proj, ba, conv_w, alog_row, dtb_row)


def _gdn_scan_kernel(u_ref, w_ref, qd_ref, ai_ref, kdt_ref, gl_ref, z_ref, nw_ref, o_ref, st_ref):
    c = GDN_CHUNK

    @pl.when(pl.program_id(1) == 0)
    def _():
        st_ref[...] = jnp.zeros_like(st_ref)

    zeros = jnp.zeros((c, LANES), F32)
    nw = nw_ref[...]
    heads = range(GDN_HEADS)
    sls = [slice(h * LANES, (h + 1) * LANES) for h in heads]
    states = [st_ref[h] for h in heads]
    for ci in range(2 * gl_ref.shape[0]):
        pair, half = divmod(ci, 2)
        rs = slice(ci * c, (ci + 1) * c)
        ps = slice(pair * GDN_PAIR, (pair + 1) * GDN_PAIR)
        sbs = [s.astype(BF16) for s in states]
        ws = [_dot(w_ref[rs, sls[h]], sbs[h]) for h in heads]
        qs = [_dot(qd_ref[rs, sls[h]], sbs[h]) for h in heads]
        vpads = []
        for h in heads:
            vnew = u_ref[rs, sls[h]] - ws[h]
            vpads.append(jnp.concatenate([vnew, zeros] if half == 0 else [zeros, vnew], axis=0).astype(BF16))
        outs = [qs[h] + _dot(ai_ref[pair * c:(pair + 1) * c, sls[h]], vpads[h]) for h in heads]
        states = [states[h] * gl_ref[pair, half:half + 1, sls[h]] + _dot(kdt_ref[ps, sls[h]], vpads[h])
                  for h in heads]
        for h in heads:
            o = outs[h]
            z = z_ref[rs, sls[h]].astype(F32)
            o = o * lax.rsqrt(jnp.mean(o * o, axis=-1, keepdims=True) + 1e-6) * nw * _silu(z)
            o_ref[rs, sls[h]] = o.astype(o_ref.dtype)
    for h in heads:
        st_ref[h] = states[h]


def _gdn_scan(u, w, qd, ai, kdt, gl, proj, z_col, norm_w_row, batch, seq, pairs):
    t = u.shape[0]
    gw = GDN_HEADS * GDN_HEAD_DIM
    n = pairs * GDN_PAIR
    ppb = seq // n
    row_spec = pl.BlockSpec((n, gw), lambda b, i: (b * ppb + i, 0))
    return pl.pallas_call(
        _gdn_scan_kernel,
        grid=(batch, ppb),
        in_specs=[row_spec, row_spec, row_spec,
                  pl.BlockSpec((n // 2, gw), lambda b, i: (b * ppb + i, 0)), row_spec,
                  pl.BlockSpec((pairs, 2, gw), lambda b, i: (b * ppb + i, 0, 0)),
                  pl.BlockSpec((n, gw), lambda b, i: (b * ppb + i, z_col)),
                  pl.BlockSpec((1, LANES), lambda b, i: (0, 0))],
        out_specs=row_spec,
        out_shape=jax.ShapeDtypeStruct((t, gw), BF16),
        scratch_shapes=[pltpu.VMEM((GDN_HEADS, GDN_HEAD_DIM, GDN_HEAD_DIM), F32)],
        compiler_params=_params("arbitrary", "arbitrary"),
        name="gdn_scan",
    )(u, w, qd, ai, kdt, gl, proj, norm_w_row)


def _merge_kernel(x_ref, a_ref, g_ref, wgs_ref, wgg_ref, wbs_ref, wbg_ref, o_ref):
    x = x_ref[...]
    gate_s = _sigmoid(jnp.dot(x, wgs_ref[...], preferred_element_type=F32))
    gate_g = _sigmoid(jnp.dot(x, wgg_ref[...], preferred_element_type=F32))
    y_s = jnp.dot(a_ref[...], wbs_ref[...], preferred_element_type=F32)
    y_g = jnp.dot(g_ref[...], wbg_ref[...], preferred_element_type=F32)
    o_ref[...] = (gate_s * y_s + gate_g * y_g).astype(o_ref.dtype)


def _merge(xb, attn, gdn, wgs, wgg, wbs, wbg, tm, tn):
    t, d = xb.shape
    ws = attn.shape[1]
    wg = gdn.shape[1]

    def wspec(k):
        return pl.BlockSpec((k, tn), lambda i, j: (0, j))

    return pl.pallas_call(
        _merge_kernel,
        grid=(t // tm, d // tn),
        in_specs=[pl.BlockSpec((tm, d), lambda i, j: (i, 0)),
                  pl.BlockSpec((tm, ws), lambda i, j: (i, 0)),
                  pl.BlockSpec((tm, wg), lambda i, j: (i, 0)),
                  wspec(d), wspec(d), wspec(ws), wspec(wg)],
        out_specs=pl.BlockSpec((tm, tn), lambda i, j: (i, j)),
        out_shape=jax.ShapeDtypeStruct((t, d), BF16),
        compiler_params=_params("parallel", "arbitrary"),
        name="branch_merge",
    )(xb, attn, gdn, wgs, wgg, wbs, wbg)


def _mixo_ln_kernel(m_ref, x_ref, w_ref, g_ref, b_ref, o_ref, *, alpha):
    y = alpha * x_ref[...] + jnp.dot(m_ref[...], w_ref[...], preferred_element_type=F32)
    o_ref[...] = _layer_norm(y, g_ref[...], b_ref[...])


def _mixo_ln(mixed, x, w, g, b, alpha, tm):
    t, d = x.shape
    vec = pl.BlockSpec((1, d), lambda i: (0, 0))
    return pl.pallas_call(
        functools.partial(_mixo_ln_kernel, alpha=alpha),
        grid=(t // tm,),
        in_specs=[pl.BlockSpec((tm, d), lambda i: (i, 0)),
                  pl.BlockSpec((tm, d), lambda i: (i, 0)),
                  pl.BlockSpec((d, d), lambda i: (0, 0)),
                  vec, vec],
        out_specs=pl.BlockSpec((tm, d), lambda i: (i, 0)),
        out_shape=jax.ShapeDtypeStruct((t, d), F32),
        compiler_params=_params("parallel"),
        name="mixo_layernorm",
    )(mixed, x, w, g, b)


def _mem_kernel(x_ref, wq_ref, kv_ref, wo_ref, g_ref, b_ref, o_ref, *, alpha):
    x = x_ref[...]
    q = jnp.dot(x.astype(BF16), wq_ref[...], preferred_element_type=F32)
    mw = MEM_HEADS * MEM_HEAD_DIM
    outs = []
    qb = q.astype(BF16)
    scores = [_dot_nt(qb[:, h * MEM_HEAD_DIM:(h + 1) * MEM_HEAD_DIM],
                      kv_ref[:, h * MEM_HEAD_DIM:(h + 1) * MEM_HEAD_DIM]) for h in range(MEM_HEADS)]
    es, inv = [], []
    for h in range(MEM_HEADS):
        s = scores[h] * (MEM_HEAD_DIM ** -0.5)
        m = jnp.max(s, axis=-1, keepdims=True)
        e = jnp.exp(s - m)
        inv.append(1.0 / jnp.sum(e, axis=-1, keepdims=True))
        es.append(e.astype(BF16))
    for h in range(MEM_HEADS):
        vh = kv_ref[:, mw + h * MEM_HEAD_DIM:mw + (h + 1) * MEM_HEAD_DIM]
        outs.append(_dot(es[h], vh) * inv[h])
    o = jnp.concatenate(outs, axis=1)
    y = alpha * x + _dot(o, wo_ref[...])
    o_ref[...] = _layer_norm(y, g_ref[...], b_ref[...])


def _mem_attention_ln(x, wq, kv, wo, g, b, alpha, batch, seq, mem_len, tm):
    t, d = x.shape
    mw = MEM_HEADS * MEM_HEAD_DIM
    tpb = seq // tm
    vec = pl.BlockSpec((1, d), lambda bi, i: (0, 0))
    return pl.pallas_call(
        functools.partial(_mem_kernel, alpha=alpha),
        grid=(batch, tpb),
        in_specs=[pl.BlockSpec((tm, d), lambda bi, i: (bi * tpb + i, 0)),
                  pl.BlockSpec((d, mw), lambda bi, i: (0, 0)),
                  pl.BlockSpec((mem_len, 2 * mw), lambda bi, i: (bi, 0)),
                  pl.BlockSpec((mw, d), lambda bi, i: (0, 0)),
                  vec, vec],
        out_specs=pl.BlockSpec((tm, d), lambda bi, i: (bi * tpb + i, 0)),
        out_shape=jax.ShapeDtypeStruct((t, d), F32),
        compiler_params=_params("parallel", "arbitrary"),
        name="mem_attention_layernorm",
    )(x, wq, kv, wo, g, b)


def _ffn_kernel(x_ref, halo_ref, wg_ref, wu_ref, cg_ref, cu_ref, bg_ref, bu_ref, wd_ref, g_ref, b_ref,
                o_ref, xb_ref, *, alpha, tiles_per_batch):
    j = pl.program_id(1)
    hr = BF16_ROWS

    @pl.when(j == 0)
    def _():
        first = (pl.program_id(0) % tiles_per_batch) == 0
        xb_ref[:hr, :] = jnp.where(first, 0.0, halo_ref[...]).astype(BF16)
        xb_ref[hr:, :] = x_ref[...].astype(BF16)
        o_ref[...] = jnp.zeros_like(o_ref)

    xb = xb_ref[...]

    def conv(w_ref, cw_ref, bias_ref):
        h = jnp.dot(xb, w_ref[...], preferred_element_type=F32)
        cw = cw_ref[...]
        y = cw[FFN_CONV - 1:FFN_CONV] * h[hr:] + bias_ref[...]
        for k in range(1, FFN_CONV):
            y = y + cw[FFN_CONV - 1 - k:FFN_CONV - k] * pltpu.roll(h, k, axis=0)[hr:]
        return y

    act = _silu(conv(wg_ref, cg_ref, bg_ref)) * conv(wu_ref, cu_ref, bu_ref)
    o_ref[...] += _dot(act, wd_ref[...])

    @pl.when(j == pl.num_programs(1) - 1)
    def _():
        y = alpha * x_ref[...] + o_ref[...]
        o_ref[...] = _layer_norm(y, g_ref[...], b_ref[...])


def _ffn_ln(x, wg, wu, cg, cu, bg, bu, wd, g, b, alpha, seq, tm, tn):
    t, d = x.shape
    ff = wg.shape[1]
    tpb = seq // tm
    hb = tm // BF16_ROWS
    vec = pl.BlockSpec((1, d), lambda i, j: (0, 0))
    return pl.pallas_call(
        functools.partial(_ffn_kernel, alpha=alpha, tiles_per_batch=tpb),
        grid=(t // tm, ff // tn),
        in_specs=[pl.BlockSpec((tm, d), lambda i, j: (i, 0), pipeline_mode=pl.Buffered(1)),
                  pl.BlockSpec((BF16_ROWS, d), lambda i, j: (jnp.maximum(i * hb - 1, 0), 0)),
                  pl.BlockSpec((d, tn), lambda i, j: (0, j)),
                  pl.BlockSpec((d, tn), lambda i, j: (0, j)),
                  pl.BlockSpec((FFN_CONV, tn), lambda i, j: (0, j)),
                  pl.BlockSpec((FFN_CONV, tn), lambda i, j: (0, j)),
                  pl.BlockSpec((1, tn), lambda i, j: (0, j)),
                  pl.BlockSpec((1, tn), lambda i, j: (0, j)),
                  pl.BlockSpec((tn, d), lambda i, j: (j, 0)),
                  vec, vec],
        out_specs=pl.BlockSpec((tm, d), lambda i, j: (i, 0)),
        out_shape=jax.ShapeDtypeStruct((t, d), F32),
        scratch_shapes=[pltpu.VMEM((BF16_ROWS + tm, d), BF16)],
        compiler_params=_params("parallel", "arbitrary"),
        name="ffn_layernorm",
    )(x, x, wg, wu, cg, cu, bg, bu, wd, g, b)


def _pad_cols(a, n):
    return jnp.pad(a, ((0, 0), (0, n - a.shape[1])))


def _layer(x2d, mem2d, batch, seq, mem_len, p, bias_tab, alpha, tiles):
    d = x2d.shape[1]
    sq = SWA_HEADS * SWA_HEAD_DIM
    skv = SWA_KV_HEADS * SWA_HEAD_DIM
    gw = GDN_HEADS * GDN_HEAD_DIM
    w_in = p["w_in"]
    o_gdn = sq + 2 * skv
    o_ba = o_gdn + 4 * gw
    o_gate = o_ba + 2 * GDN_HEADS
    w_proj = jnp.concatenate([w_in[:, o_gdn:o_ba], w_in[:, :sq] * (SWA_HEAD_DIM ** -0.5),
                              w_in[:, sq:o_gdn]], axis=1).astype(BF16)
    w_ba = _pad_cols(w_in[:, o_ba:o_gate], LANES).astype(BF16)
    w_gs = w_in[:, o_gate:o_gate + d].astype(BF16)
    w_gg = w_in[:, o_gate + d:o_gate + 2 * d].astype(BF16)

    proj, xb, ba = _proj(x2d, w_proj, w_ba, tiles["proj_tm"], tiles["proj_tn"])

    attn = _swa_attention(proj, (4 * gw) // sq, (4 * gw + sq) // (2 * skv), p["swa_sinks"], bias_tab, seq,
                          tiles["swa_blocks"])

    zpad = jnp.zeros((GDN_HEADS,), F32)
    alog_row = _pad_cols(jnp.concatenate([zpad, p["gdn_a_log"]])[None, :], LANES)
    dtb_row = _pad_cols(jnp.concatenate([zpad, p["gdn_dt_bias"]])[None, :], LANES)
    u, w, qd, ai, kdt, gl = _gdn_prepass(proj, ba, p["gdn_conv_w"], alog_row, dtb_row, seq)
    gdn = _gdn_scan(u, w, qd, ai, kdt, gl, proj, 3, p["gdn_norm_w"][None, :], batch, seq,
                    tiles["scan_pairs"])

    mixed = _merge(xb, attn, gdn, w_gs, w_gg, p["w_br_swa"].astype(BF16), p["w_br_gdn"].astype(BF16),
                   tiles["merge_tm"], tiles["merge_tn"])
    x1 = _mixo_ln(mixed, x2d, p["w_mix_o"].astype(BF16), p["ln1_g"][None, :], p["ln1_b"][None, :],
                  alpha, tiles["ln_tm"])

    kv = _matmul(mem2d.astype(BF16), p["w_mem_kv"].astype(BF16), mem_len, tiles["kv_tn"], BF16)
    x2 = _mem_attention_ln(x1, p["w_mem_q"].astype(BF16), kv, p["w_mem_o"].astype(BF16),
                           p["ln2_g"][None, :], p["ln2_b"][None, :], alpha, batch, seq, mem_len,
                           tiles["mem_tm"])

    dff = p["w_down"].shape[0]
    tn = tiles["ffn_tn"]
    ffp = -(-dff // tn) * tn
    w_up = p["w_up"]
    cw = p["ffn_conv_w"]
    cb = p["ffn_conv_b"][None, :]
    x3 = _ffn_ln(x2,
                 _pad_cols(w_up[:, :dff], ffp).astype(BF16), _pad_cols(w_up[:, dff:], ffp).astype(BF16),
                 _pad_cols(cw[:, :dff], ffp), _pad_cols(cw[:, dff:], ffp),
                 _pad_cols(cb[:, :dff], ffp), _pad_cols(cb[:, dff:], ffp),
                 jnp.pad(p["w_down"], ((0, ffp - dff), (0, 0))).astype(BF16),
                 p["ln3_g"][None, :], p["ln3_b"][None, :], alpha, seq, tiles["ffn_tm"], tn)
    return x3


_TILES = dict(proj_tm=1024, proj_tn=1792, merge_tm=1024, merge_tn=1024, ln_tm=512, kv_tn=512,
              mem_tm=1024, ffn_tm=1024, ffn_tn=512, swa_blocks=2, scan_pairs=2)

_PER_LAYER = ("w_in", "swa_sinks", "gdn_conv_w", "gdn_a_log", "gdn_dt_bias", "gdn_norm_w", "w_br_swa",
              "w_br_gdn", "w_mix_o", "ln1_g", "ln1_b", "w_mem_q", "w_mem_kv", "w_mem_o", "ln2_g", "ln2_b",
              "w_up", "ffn_conv_w", "ffn_conv_b", "w_down", "ln3_g", "ln3_b")


def _forward(x, mem, rel_bias, weights, tiles):
    batch, seq, d = x.shape
    mem_len = mem.shape[1]
    depth = weights["w_in"].shape[0]
    alpha = (2 * depth) ** 0.25
    bias_tab = _swa_bias_table(rel_bias)
    x2d = x.reshape(batch * seq, d)
    mem2d = mem.reshape(batch * mem_len, d)
    for l in range(depth):
        p = {name: weights[name][l] for name in _PER_LAYER}
        x2d = _layer(x2d, mem2d, batch, seq, mem_len, p, bias_tab, alpha, tiles)
    return x2d.reshape(batch, seq, d)


def kernel(x, mem, w_in, rel_bias, swa_sinks, gdn_conv_w, gdn_a_log, gdn_dt_bias, gdn_norm_w, w_br_swa, w_br_gdn, w_mix_o, ln1_g, ln1_b, w_mem_q, w_mem_kv, w_mem_o, ln2_g, ln2_b, w_up, ffn_conv_w, ffn_conv_b, w_down, ln3_g, ln3_b):
    weights = dict(w_in=w_in, swa_sinks=swa_sinks, gdn_conv_w=gdn_conv_w, gdn_a_log=gdn_a_log,
                   gdn_dt_bias=gdn_dt_bias, gdn_norm_w=gdn_norm_w, w_br_swa=w_br_swa, w_br_gdn=w_br_gdn,
                   w_mix_o=w_mix_o, ln1_g=ln1_g, ln1_b=ln1_b, w_mem_q=w_mem_q, w_mem_kv=w_mem_kv,
                   w_mem_o=w_mem_o, ln2_g=ln2_g, ln2_b=ln2_b, w_up=w_up, ffn_conv_w=ffn_conv_w,
                   ffn_conv_b=ffn_conv_b, w_down=w_down, ln3_g=ln3_g, ln3_b=ln3_b)
    return _forward(x, mem, rel_bias, weights, _TILES)
```

```python
import functools
import math

import jax
import jax.numpy as jnp
from jax import lax
from jax.experimental import pallas as pl
from jax.experimental.pallas import tpu as pltpu

F32 = jnp.float32
BF16 = jnp.bfloat16

LANES = 128
BF16_ROWS = 16
VMEM_LIMIT = 56 * 1024 * 1024

SWA_HEADS = 16
SWA_KV_HEADS = 2
SWA_HEAD_DIM = 64
SWA_BLOCK = 128
REL_BUCKETS = 32
REL_MAX_DIST = 128
GDN_HEADS = 8
GDN_HEAD_DIM = 128
GDN_CONV = 4
GDN_CHUNK = 64
GDN_PAIR = 2 * GDN_CHUNK
MEM_HEADS = 4
MEM_HEAD_DIM = 128
FFN_CONV = 3
NORM_EPS = 1e-5
NEG_INF = -1e30


def _params(*sem):
    return pltpu.CompilerParams(dimension_semantics=sem, vmem_limit_bytes=VMEM_LIMIT)


def _dot(a, b):
    return jnp.dot(a.astype(BF16), b.astype(BF16), preferred_element_type=F32)


def _dot_nt(a, b):
    return lax.dot_general(a.astype(BF16), b.astype(BF16), (((1,), (1,)), ((), ())),
                           preferred_element_type=F32)


def _split(a):
    hi = a.astype(BF16)
    lo = (a - hi.astype(F32)).astype(BF16)
    return hi, lo


def _dot3(a, b):
    ah, al = _split(a)
    bh, bl = _split(b)
    d = functools.partial(jnp.dot, preferred_element_type=F32)
    return d(ah, bh) + (d(al, bh) + d(ah, bl))


def _sigmoid(x):
    return 1.0 / (1.0 + jnp.exp(-x))


def _silu(x):
    return x * _sigmoid(x)


def _layer_norm(y, g, b):
    mu = jnp.mean(y, axis=-1, keepdims=True)
    yc = y - mu
    var = jnp.mean(yc * yc, axis=-1, keepdims=True)
    return yc * lax.rsqrt(var + NORM_EPS) * g + b


def _mm_kernel(x_ref, w_ref, o_ref):
    o_ref[...] = jnp.dot(x_ref[...], w_ref[...], preferred_element_type=F32).astype(o_ref.dtype)


def _matmul(x, w, tm, tn, out_dtype):
    m, k = x.shape
    n = w.shape[1]
    assert m % tm == 0 and n % tn == 0
    return pl.pallas_call(
        _mm_kernel,
        grid=(m // tm, n // tn),
        in_specs=[pl.BlockSpec((tm, k), lambda i, j: (i, 0)),
                  pl.BlockSpec((k, tn), lambda i, j: (0, j))],
        out_specs=pl.BlockSpec((tm, tn), lambda i, j: (i, j)),
        out_shape=jax.ShapeDtypeStruct((m, n), out_dtype),
        compiler_params=_params("parallel", "arbitrary"),
        name="matmul",
    )(x, w)


def _proj_kernel(x_ref, w_ref, ws_ref, o_ref, xb_ref, os_ref):
    @pl.when(pl.program_id(1) == 0)
    def _():
        xb_ref[...] = x_ref[...].astype(xb_ref.dtype)
        os_ref[...] = jnp.dot(xb_ref[...], ws_ref[...], preferred_element_type=F32)

    o_ref[...] = jnp.dot(xb_ref[...], w_ref[...], preferred_element_type=F32).astype(o_ref.dtype)


def _proj(x, w, w_small, tm, tn):
    m, k = x.shape
    n = w.shape[1]
    ns = w_small.shape[1]
    assert m % tm == 0 and n % tn == 0
    return pl.pallas_call(
        _proj_kernel,
        grid=(m // tm, n // tn),
        in_specs=[pl.BlockSpec((tm, k), lambda i, j: (i, 0)),
                  pl.BlockSpec((k, tn), lambda i, j: (0, j)),
                  pl.BlockSpec((k, ns), lambda i, j: (0, 0))],
        out_specs=[pl.BlockSpec((tm, tn), lambda i, j: (i, j)),
                   pl.BlockSpec((tm, k), lambda i, j: (i, 0)),
                   pl.BlockSpec((tm, ns), lambda i, j: (i, 0))],
        out_shape=[jax.ShapeDtypeStruct((m, n), BF16), jax.ShapeDtypeStruct((m, k), BF16),
                   jax.ShapeDtypeStruct((m, ns), F32)],
        compiler_params=_params("parallel", "arbitrary"),
        name="input_proj",
    )(x, w, w_small)


def _t5_causal_bucket(dist):
    max_exact = REL_BUCKETS // 2
    d = jnp.maximum(dist, 1).astype(F32)
    large = max_exact + (jnp.log(d / max_exact) / math.log(REL_MAX_DIST / max_exact)
                         * (REL_BUCKETS - max_exact)).astype(jnp.int32)
    large = jnp.minimum(large, REL_BUCKETS - 1)
    return jnp.where(dist < max_exact, dist, large)


def _swa_bias_table(rel_bias):
    nq, nk = SWA_BLOCK, 2 * SWA_BLOCK
    dist = nk - 1 - jnp.arange(nq + nk - 1)
    per_dist = rel_bias.astype(F32)[_t5_causal_bucket(jnp.maximum(dist, 0))]
    per_dist = jnp.where(((dist >= 0) & (dist < SWA_BLOCK))[:, None], per_dist, NEG_INF)
    u = jnp.pad(per_dist.T, ((0, 0), (0, 1)))
    skew = jnp.tile(u, (1, nq))[:, :nq * (nq + nk - 1)].reshape(-1, nq, nq + nk - 1)
    table = skew[:, :, nq - 1:]
    no_prev = jnp.where(jnp.arange(nk) < nq, NEG_INF, table)
    return jnp.stack([table, no_prev])


def _swa_kernel(sink_ref, q_ref, kv_ref, pkv_ref, bias0_ref, bias_ref, o_ref, *, blocks):
    half = SWA_HEAD_DIM
    kv = jnp.concatenate([pkv_ref[...], kv_ref[...]], axis=0)
    k = kv[:, :LANES]
    v = kv[:, LANES:]
    lo = lax.broadcasted_iota(jnp.int32, k.shape, 1) < half
    zero = jnp.zeros_like(k)

    def swap(t):
        return jnp.concatenate([t[:, half:], t[:, :half]], axis=1)

    def padded(t):
        tr = swap(t)
        return ((jnp.where(lo, t, zero), jnp.where(lo, zero, tr)),
                (jnp.where(lo, tr, zero), jnp.where(lo, zero, t)))

    kpad = padded(k)
    vpad = padded(v)
    out_lo = lax.broadcasted_iota(jnp.int32, (SWA_BLOCK, LANES), 1) < half
    group = SWA_HEADS // SWA_KV_HEADS
    for blk in range(blocks):
        rows = slice(blk * SWA_BLOCK, (blk + 1) * SWA_BLOCK)
        band = slice(blk * SWA_BLOCK, (blk + 2) * SWA_BLOCK)
        table_ref = bias0_ref if blk == 0 else bias_ref
        scores = [_dot_nt(q_ref[rows, (head // 2) * LANES:(head // 2 + 1) * LANES],
                          kpad[head // group][head % 2][band]) for head in range(SWA_HEADS)]
        es, inv = [], []
        for head in range(SWA_HEADS):
            s = scores[head] + table_ref[head]
            sink = sink_ref[head]
            m = jnp.maximum(jnp.max(s, axis=-1, keepdims=True), sink)
            e = jnp.exp(s - m)
            inv.append(1.0 / (jnp.sum(e, axis=-1, keepdims=True) + jnp.exp(sink - m)))
            es.append(e.astype(BF16))
        for p in range(SWA_HEADS // 2):
            acc = (_dot(es[2 * p], vpad[(2 * p) // group][0][band])
                   + _dot(es[2 * p + 1], vpad[(2 * p + 1) // group][1][band]))
            o_ref[rows, p * LANES:(p + 1) * LANES] = (
                acc * jnp.where(out_lo, inv[2 * p], inv[2 * p + 1])).astype(o_ref.dtype)


def _swa_attention(proj, q_col, kv_col, sinks, bias_tabs, seq, blocks):
    t = proj.shape[0]
    qw = SWA_HEADS * SWA_HEAD_DIM
    kvw = 2 * SWA_KV_HEADS * SWA_HEAD_DIM
    tq = blocks * SWA_BLOCK
    spb = seq // tq
    table = (None, SWA_HEADS, SWA_BLOCK, 2 * SWA_BLOCK)
    return pl.pallas_call(
        functools.partial(_swa_kernel, blocks=blocks),
        grid=(t // tq,),
        in_specs=[pl.BlockSpec(memory_space=pltpu.SMEM),
                  pl.BlockSpec((tq, qw), lambda i: (i, q_col)),
                  pl.BlockSpec((tq, kvw), lambda i: (i, kv_col)),
                  pl.BlockSpec((SWA_BLOCK, kvw), lambda i: (jnp.maximum(i * blocks - 1, 0), kv_col)),
                  pl.BlockSpec(table, lambda i: (jnp.where(i % spb == 0, 1, 0), 0, 0, 0)),
                  pl.BlockSpec(table, lambda i: (0, 0, 0, 0))],
        out_specs=pl.BlockSpec((tq, qw), lambda i: (i, 0)),
        out_shape=jax.ShapeDtypeStruct((t, qw), BF16),
        compiler_params=_params("parallel"),
        name="swa_attention",
    )(sinks, proj, proj, proj, bias_tabs, bias_tabs)


def _gdn_pre_kernel(q_ref, k_ref, v_ref, hq_ref, hk_ref, hv_ref, ba_ref, cw_ref, alog_ref, dtb_ref,
                    u_ref, w_ref, qd_ref, ai_ref, kdt_ref, gl_ref, *, steps_per_batch, pairs):
    n = GDN_PAIR
    c = GDN_CHUNK
    first_step = (pl.program_id(0) % steps_per_batch) == 0
    for pr in range(pairs):
        rows = pl.ds(pr * n, n)
        if pr == 0:
            halos, first = (hq_ref, hk_ref, hv_ref), first_step
        else:
            tail = pl.ds(pr * n - BF16_ROWS, BF16_ROWS)
            halos, first = (q_ref.at[tail], k_ref.at[tail], v_ref.at[tail]), False
        _gdn_pair(q_ref.at[rows], k_ref.at[rows], v_ref.at[rows], *halos, ba_ref.at[rows], cw_ref, alog_ref,
                  dtb_ref, u_ref.at[rows], w_ref.at[rows], qd_ref.at[rows], ai_ref.at[pl.ds(pr * c, c)],
                  kdt_ref.at[rows], gl_ref.at[pl.ds(pr, 1)], first)


def _gdn_pair(q_ref, k_ref, v_ref, hq_ref, hk_ref, hv_ref, ba_ref, cw_ref, alog_ref, dtb_ref,
              u_ref, w_ref, qd_ref, ai_ref, kdt_ref, gl_ref, first):
    n = GDN_PAIR
    c = GDN_CHUNK
    row = lax.broadcasted_iota(jnp.int32, (n, n), 0)
    col = lax.broadcasted_iota(jnp.int32, (n, n), 1)
    same = (row >= c) == (col >= c)
    tril = jnp.logical_and(same, row >= col)
    rowc = lax.broadcasted_iota(jnp.int32, (n, 1), 0)
    crow = lax.broadcasted_iota(jnp.int32, (c, n), 0)
    clane = lax.broadcasted_iota(jnp.int32, (c, n), 1)
    first_chunk = clane < c
    ccol = clane & (c - 1)
    ctril = crow >= ccol
    clower = crow > ccol
    cblk = crow ^ ccol
    ceye = jnp.where(crow == ccol, 1.0, 0.0).astype(F32)

    def compact(m):
        return jnp.where(first_chunk, m[:c], m[c:])

    def expand(m):
        z = jnp.zeros_like(m)
        return jnp.concatenate([jnp.where(first_chunk, m, z), jnp.where(first_chunk, z, m)], axis=0)

    ba = ba_ref[...]
    beta_all = _sigmoid(ba)
    x = ba + dtb_ref[...]
    softplus = jnp.maximum(x, 0.0) + jnp.log1p(jnp.exp(-jnp.abs(x)))
    g_all = -jnp.exp(alog_ref[...]) * softplus
    gc_all = _dot3(jnp.where(tril, 1.0, 0.0).astype(F32), g_all)
    gct_all = gc_all.T

    def conv_silu(cur_ref, halo_ref, which, sl):
        cur = cur_ref[:, sl].astype(F32)
        halo = jnp.where(first, 0.0, halo_ref[:, sl].astype(F32))
        ext = jnp.concatenate([halo, cur], axis=0)
        base = which * GDN_HEADS * GDN_HEAD_DIM
        cw = cw_ref[:, base + sl.start:base + sl.stop]
        y = cw[GDN_CONV - 1:GDN_CONV] * cur
        for j in range(1, GDN_CONV):
            y = y + cw[GDN_CONV - 1 - j:GDN_CONV - j] * pltpu.roll(ext, j, axis=0)[BF16_ROWS:]
        return _silu(y)

    def l2norm(t):
        return t * lax.rsqrt(jnp.sum(t * t, axis=-1, keepdims=True) + 1e-6)

    heads = range(GDN_HEADS)
    sls = [slice(h * LANES, (h + 1) * LANES) for h in heads]
    dotf = functools.partial(jnp.dot, preferred_element_type=F32)

    a_mats, rhs = [], []
    for h in heads:
        sl = sls[h]
        q = l2norm(conv_silu(q_ref, hq_ref, 0, sl)) * (GDN_HEAD_DIM ** -0.5)
        k = l2norm(conv_silu(k_ref, hk_ref, 1, sl))
        v = conv_silu(v_ref, hv_ref, 2, sl)
        beta = beta_all[:, h:h + 1]
        gc = gc_all[:, GDN_HEADS + h:GDN_HEADS + h + 1]
        gcr = gct_all[GDN_HEADS + h:GDN_HEADS + h + 1, :]
        kb = k * beta
        kq = _dot_nt(jnp.concatenate([kb, q], axis=0), k)
        gcc = jnp.where(first_chunk, gc[:c], gc[c:])
        decay = jnp.where(ctril, jnp.exp(jnp.where(ctril, gcc - gcr, 0.0)), 0.0)
        a_mats.append(jnp.where(clower, compact(kq[:n]) * decay, 0.0))
        egc = jnp.exp(gc)
        rhs.append(jnp.concatenate([v * beta, kb * egc], axis=1))
        gl_e = gc[c - 1:c, :]
        gl_o = gc[n - 1:n, :]
        kd = k * jnp.exp(jnp.where(rowc < c, gl_e, gl_o) - gc)
        qd_ref[:, sl] = (q * egc).astype(qd_ref.dtype)
        ai_ref[:, sl] = (compact(kq[n:]) * decay).astype(ai_ref.dtype)
        kdt_ref[:, sl] = kd.T.astype(kdt_ref.dtype)
        gl_ref[0, 0:1, sl] = jnp.broadcast_to(jnp.exp(gl_e), (1, LANES))
        gl_ref[0, 1:2, sl] = jnp.broadcast_to(jnp.exp(gl_o), (1, LANES))

    def dot3(a_split, b_split):
        (ah, al), (bh, bl) = a_split, b_split
        return dotf(jnp.concatenate([ah, al, ah], axis=1), jnp.concatenate([bh, bh, bl], axis=0))

    def expand_split(s):
        return tuple(expand(t) for t in s)

    base_blk = 8
    diag = jnp.logical_and(clower, cblk < base_blk)
    zero_bf = jnp.zeros((c, n), BF16)
    a_splits = [_split(a) for a in a_mats]
    d_splits = [tuple(jnp.where(diag, t, zero_bf) for t in s) for s in a_splits]
    d2s = [dot3(d, expand_split(d)) for d in d_splits]
    xs = []
    for h in heads:
        p = ceye - jnp.where(diag, a_mats[h], 0.0)
        d2 = _split(d2s[h])
        ps_ = _split(p)
        m = dot3(tuple(jnp.concatenate([s, t], axis=0) for s, t in zip(d2, ps_)), expand_split(d2))
        d2s[h] = m[:c]
        xs.append(p + m[c:])
    for h in heads:
        xs[h] = xs[h] + dot3(_split(xs[h]), expand_split(_split(d2s[h])))
    b = base_blk
    while b < c:
        off = jnp.logical_and(clower, jnp.logical_and(cblk >= b, cblk < 2 * b))
        o_splits = [tuple(jnp.where(off, t, zero_bf) for t in s) for s in a_splits]
        x_splits = [_split(x) for x in xs]
        ys = [dot3(o_splits[h], expand_split(x_splits[h])) for h in heads]
        for h in heads:
            xs[h] = xs[h] - dot3(x_splits[h], expand_split(_split(ys[h])))
        b *= 2
    zero_rhs = jnp.zeros((c, 2 * LANES), BF16)

    def chunk_diag(t):
        return jnp.concatenate([jnp.concatenate([t[:c], zero_rhs], axis=1),
                                jnp.concatenate([zero_rhs, t[c:]], axis=1)], axis=0)

    for h in heads:
        sol = dot3(_split(xs[h]), tuple(chunk_diag(t) for t in _split(rhs[h])))
        for ci in range(2):
            rs = slice(ci * c, (ci + 1) * c)
            u_ref[rs, sls[h]] = sol[:, (2 * ci) * LANES:(2 * ci + 1) * LANES]
            w_ref[rs, sls[h]] = sol[:, (2 * ci + 1) * LANES:(2 * ci + 2) * LANES].astype(w_ref.dtype)


def _gdn_prepass(proj, ba, conv_w, alog_row, dtb_row, seq, pairs):
    t = proj.shape[0]
    gw = GDN_HEADS * GDN_HEAD_DIM
    n = pairs * GDN_PAIR
    spb = seq // n
    hb = n // BF16_ROWS

    def cur(cb):
        return pl.BlockSpec((n, gw), lambda i: (i, cb))

    def halo(cb):
        return pl.BlockSpec((BF16_ROWS, gw), lambda i: (jnp.maximum(i * hb - 1, 0), cb))

    row_spec = pl.BlockSpec((n, gw), lambda i: (i, 0))
    return pl.pallas_call(
        functools.partial(_gdn_pre_kernel, steps_per_batch=spb, pairs=pairs),
        grid=(t // n,),
        in_specs=[cur(0), cur(1), cur(2), halo(0), halo(1), halo(2),
                  pl.BlockSpec((n, LANES), lambda i: (i, 0)),
                  pl.BlockSpec((GDN_CONV, 3 * gw), lambda i: (0, 0)),
                  pl.BlockSpec((1, LANES), lambda i: (0, 0)),
                  pl.BlockSpec((1, LANES), lambda i: (0, 0))],
        out_specs=[row_spec, row_spec, row_spec,
                   pl.BlockSpec((n // 2, gw), lambda i: (i, 0)), row_spec,
                   pl.BlockSpec((pairs, 2, gw), lambda i: (i, 0, 0))],
        out_shape=[jax.ShapeDtypeStruct((t, gw), F32),
                   jax.ShapeDtypeStruct((t, gw), BF16),
                   jax.ShapeDtypeStruct((t, gw), BF16),
                   jax.ShapeDtypeStruct((t // 2, gw), BF16),
                   jax.ShapeDtypeStruct((t, gw), BF16),
                   jax.ShapeDtypeStruct((t // GDN_PAIR, 2, gw), F32)],
        compiler_params=_params("parallel"),
        name="gdn_prepass",
    )(proj, proj, proj, proj, proj, proj, ba, conv_w, alog_row, dtb_row)


def _gdn_scan_kernel(u_ref, w_ref, qd_ref, ai_ref, kdt_ref, gl_ref, z_ref, nw_ref, o_ref, st_ref):
    c = GDN_CHUNK

    @pl.when(pl.program_id(1) == 0)
    def _():
        st_ref[...] = jnp.zeros_like(st_ref)

    zeros = jnp.zeros((c, LANES), F32)
    nw = nw_ref[...]
    heads = range(GDN_HEADS)
    sls = [slice(h * LANES, (h + 1) * LANES) for h in heads]
    states = [st_ref[h] for h in heads]
    for ci in range(2 * gl_ref.shape[0]):
        pair, half = divmod(ci, 2)
        rs = slice(ci * c, (ci + 1) * c)
        ps = slice(pair * GDN_PAIR, (pair + 1) * GDN_PAIR)
        sbs = [s.astype(BF16) for s in states]
        ws = [_dot(w_ref[rs, sls[h]], sbs[h]) for h in heads]
        qs = [_dot(qd_ref[rs, sls[h]], sbs[h]) for h in heads]
        vpads = []
        for h in heads:
            vnew = u_ref[rs, sls[h]] - ws[h]
            vpads.append(jnp.concatenate([vnew, zeros] if half == 0 else [zeros, vnew], axis=0).astype(BF16))
        outs = [qs[h] + _dot(ai_ref[pair * c:(pair + 1) * c, sls[h]], vpads[h]) for h in heads]
        states = [states[h] * gl_ref[pair, half:half + 1, sls[h]] + _dot(kdt_ref[ps, sls[h]], vpads[h])
                  for h in heads]
        for h in heads:
            o = outs[h]
            z = z_ref[rs, sls[h]].astype(F32)
            o = o * lax.rsqrt(jnp.mean(o * o, axis=-1, keepdims=True) + 1e-6) * nw * _silu(z)
            o_ref[rs, sls[h]] = o.astype(o_ref.dtype)
    for h in heads:
        st_ref[h] = states[h]


def _gdn_scan(u, w, qd, ai, kdt, gl, proj, z_col, norm_w_row, batch, seq, pairs):
    t = u.shape[0]
    gw = GDN_HEADS * GDN_HEAD_DIM
    n = pairs * GDN_PAIR
    ppb = seq // n
    row_spec = pl.BlockSpec((n, gw), lambda b, i: (b * ppb + i, 0))
    return pl.pallas_call(
        _gdn_scan_kernel,
        grid=(batch, ppb),
        in_specs=[row_spec, row_spec, row_spec,
                  pl.BlockSpec((n // 2, gw), lambda b, i: (b * ppb + i, 0)), row_spec,
                  pl.BlockSpec((pairs, 2, gw), lambda b, i: (b * ppb + i, 0, 0)),
                  pl.BlockSpec((n, gw), lambda b, i: (b * ppb + i, z_col)),
                  pl.BlockSpec((1, LANES), lambda b, i: (0, 0))],
        out_specs=row_spec,
        out_shape=jax.ShapeDtypeStruct((t, gw), BF16),
        scratch_shapes=[pltpu.VMEM((GDN_HEADS, GDN_HEAD_DIM, GDN_HEAD_DIM), F32)],
        compiler_params=_params("arbitrary", "arbitrary"),
        name="gdn_scan",
    )(u, w, qd, ai, kdt, gl, proj, norm_w_row)


def _merge_kernel(x_ref, a_ref, g_ref, wgs_ref, wgg_ref, wbs_ref, wbg_ref, o_ref):
    x = x_ref[...]
    gate_s = _sigmoid(jnp.dot(x, wgs_ref[...], preferred_element_type=F32))
    gate_g = _sigmoid(jnp.dot(x, wgg_ref[...], preferred_element_type=F32))
    y_s = jnp.dot(a_ref[...], wbs_ref[...], preferred_element_type=F32)
    y_g = jnp.dot(g_ref[...], wbg_ref[...], preferred_element_type=F32)
    o_ref[...] = (gate_s * y_s + gate_g * y_g).astype(o_ref.dtype)


def _merge(xb, attn, gdn, wgs, wgg, wbs, wbg, tm, tn):
    t, d = xb.shape
    ws = attn.shape[1]
    wg = gdn.shape[1]

    def wspec(k):
        return pl.BlockSpec((k, tn), lambda i, j: (0, j))

    return pl.pallas_call(
        _merge_kernel,
        grid=(t // tm, d // tn),
        in_specs=[pl.BlockSpec((tm, d), lambda i, j: (i, 0)),
                  pl.BlockSpec((tm, ws), lambda i, j: (i, 0)),
                  pl.BlockSpec((tm, wg), lambda i, j: (i, 0)),
                  wspec(d), wspec(d), wspec(ws), wspec(wg)],
        out_specs=pl.BlockSpec((tm, tn), lambda i, j: (i, j)),
        out_shape=jax.ShapeDtypeStruct((t, d), BF16),
        compiler_params=_params("parallel", "arbitrary"),
        name="branch_merge",
    )(xb, attn, gdn, wgs, wgg, wbs, wbg)


def _mix_mem_kernel(m_ref, x_ref, wmix_ref, g1_ref, b1_ref, wq_ref, kv_ref, wo_ref, g2_ref, b2_ref, o_ref,
                    *, alpha):
    y = alpha * x_ref[...] + jnp.dot(m_ref[...], wmix_ref[...], preferred_element_type=F32)
    x = _layer_norm(y, g1_ref[...], b1_ref[...])
    q = jnp.dot(x.astype(BF16), wq_ref[...], preferred_element_type=F32)
    mw = MEM_HEADS * MEM_HEAD_DIM
    outs = []
    qb = q.astype(BF16)
    scores = [_dot_nt(qb[:, h * MEM_HEAD_DIM:(h + 1) * MEM_HEAD_DIM],
                      kv_ref[:, h * MEM_HEAD_DIM:(h + 1) * MEM_HEAD_DIM]) for h in range(MEM_HEADS)]
    es, inv = [], []
    for h in range(MEM_HEADS):
        s = scores[h] * (MEM_HEAD_DIM ** -0.5)
        m = jnp.max(s, axis=-1, keepdims=True)
        e = jnp.exp(s - m)
        inv.append(1.0 / jnp.sum(e, axis=-1, keepdims=True))
        es.append(e.astype(BF16))
    for h in range(MEM_HEADS):
        vh = kv_ref[:, mw + h * MEM_HEAD_DIM:mw + (h + 1) * MEM_HEAD_DIM]
        outs.append(_dot(es[h], vh) * inv[h])
    o = jnp.concatenate(outs, axis=1)
    y = alpha * x + _dot(o, wo_ref[...])
    o_ref[...] = _layer_norm(y, g2_ref[...], b2_ref[...])


def _mix_mem_ln(mixed, x, wmix, g1, b1, wq, kv, wo, g2, b2, alpha, batch, seq, mem_len, tm):
    t, d = x.shape
    mw = MEM_HEADS * MEM_HEAD_DIM
    tpb = seq // tm
    vec = pl.BlockSpec((1, d), lambda bi, i: (0, 0))
    rows = pl.BlockSpec((tm, d), lambda bi, i: (bi * tpb + i, 0))

    def resident(shape):
        return pl.BlockSpec(shape, lambda bi, i: (0, 0), pipeline_mode=pl.Buffered(1))

    return pl.pallas_call(
        functools.partial(_mix_mem_kernel, alpha=alpha),
        grid=(batch, tpb),
        in_specs=[rows, rows, resident((d, d)), vec, vec,
                  resident((d, mw)),
                  pl.BlockSpec((mem_len, 2 * mw), lambda bi, i: (bi, 0)),
                  resident((mw, d)),
                  vec, vec],
        out_specs=rows,
        out_shape=jax.ShapeDtypeStruct((t, d), F32),
        compiler_params=_params("parallel", "arbitrary"),
        name="mixo_mem_attention_layernorm",
    )(mixed, x, wmix, g1, b1, wq, kv, wo, g2, b2)


def _ffn_kernel(x_ref, halo_ref, wg_ref, wu_ref, cg_ref, cu_ref, bg_ref, bu_ref, wd_ref, g_ref, b_ref,
                o_ref, xb_ref, *, alpha, tiles_per_batch):
    j = pl.program_id(1)
    hr = BF16_ROWS

    @pl.when(j == 0)
    def _():
        first = (pl.program_id(0) % tiles_per_batch) == 0
        xb_ref[:hr, :] = jnp.where(first, 0.0, halo_ref[...]).astype(BF16)
        xb_ref[hr:, :] = x_ref[...].astype(BF16)
        o_ref[...] = jnp.zeros_like(o_ref)

    xb = xb_ref[...]

    def conv(w_ref, cw_ref, bias_ref):
        h = jnp.dot(xb, w_ref[...], preferred_element_type=F32)
        cw = cw_ref[...]
        y = cw[FFN_CONV - 1:FFN_CONV] * h[hr:] + bias_ref[...]
        for k in range(1, FFN_CONV):
            y = y + cw[FFN_CONV - 1 - k:FFN_CONV - k] * pltpu.roll(h, k, axis=0)[hr:]
        return y

    act = _silu(conv(wg_ref, cg_ref, bg_ref)) * conv(wu_ref, cu_ref, bu_ref)
    o_ref[...] += _dot(act, wd_ref[...])

    @pl.when(j == pl.num_programs(1) - 1)
    def _():
        y = alpha * x_ref[...] + o_ref[...]
        o_ref[...] = _layer_norm(y, g_ref[...], b_ref[...])


def _ffn_ln(x, wg, wu, cg, cu, bg, bu, wd, g, b, alpha, seq, tm, tn):
    t, d = x.shape
    ff = wg.shape[1]
    tpb = seq // tm
    hb = tm // BF16_ROWS
    vec = pl.BlockSpec((1, d), lambda i, j: (0, 0))
    return pl.pallas_call(
        functools.partial(_ffn_kernel, alpha=alpha, tiles_per_batch=tpb),
        grid=(t // tm, ff // tn),
        in_specs=[pl.BlockSpec((tm, d), lambda i, j: (i, 0), pipeline_mode=pl.Buffered(1)),
                  pl.BlockSpec((BF16_ROWS, d), lambda i, j: (jnp.maximum(i * hb - 1, 0), 0)),
                  pl.BlockSpec((d, tn), lambda i, j: (0, j)),
                  pl.BlockSpec((d, tn), lambda i, j: (0, j)),
                  pl.BlockSpec((FFN_CONV, tn), lambda i, j: (0, j)),
                  pl.BlockSpec((FFN_CONV, tn), lambda i, j: (0, j)),
                  pl.BlockSpec((1, tn), lambda i, j: (0, j)),
                  pl.BlockSpec((1, tn), lambda i, j: (0, j)),
                  pl.BlockSpec((tn, d), lambda i, j: (j, 0)),
                  vec, vec],
        out_specs=pl.BlockSpec((tm, d), lambda i, j: (i, 0)),
        out_shape=jax.ShapeDtypeStruct((t, d), F32),
        scratch_shapes=[pltpu.VMEM((BF16_ROWS + tm, d), BF16)],
        compiler_params=_params("parallel", "arbitrary"),
        name="ffn_layernorm",
    )(x, x, wg, wu, cg, cu, bg, bu, wd, g, b)


def _pad_cols(a, n):
    return jnp.pad(a, ((0, 0), (0, n - a.shape[1])))


def _layer(x2d, mem2d, batch, seq, mem_len, p, bias_tab, alpha, tiles):
    d = x2d.shape[1]
    sq = SWA_HEADS * SWA_HEAD_DIM
    skv = SWA_KV_HEADS * SWA_HEAD_DIM
    gw = GDN_HEADS * GDN_HEAD_DIM
    w_in = p["w_in"]
    o_gdn = sq + 2 * skv
    o_ba = o_gdn + 4 * gw
    o_gate = o_ba + 2 * GDN_HEADS
    w_proj = jnp.concatenate([w_in[:, o_gdn:o_ba], w_in[:, :sq] * (SWA_HEAD_DIM ** -0.5),
                              w_in[:, sq:o_gdn]], axis=1).astype(BF16)
    w_ba = _pad_cols(w_in[:, o_ba:o_gate], LANES).astype(BF16)
    w_gs = w_in[:, o_gate:o_gate + d].astype(BF16)
    w_gg = w_in[:, o_gate + d:o_gate + 2 * d].astype(BF16)

    proj, xb, ba = _proj(x2d, w_proj, w_ba, tiles["proj_tm"], tiles["proj_tn"])

    attn = _swa_attention(proj, (4 * gw) // sq, (4 * gw + sq) // (2 * skv), p["swa_sinks"], bias_tab, seq,
                          tiles["swa_blocks"])

    zpad = jnp.zeros((GDN_HEADS,), F32)
    alog_row = _pad_cols(jnp.concatenate([zpad, p["gdn_a_log"]])[None, :], LANES)
    dtb_row = _pad_cols(jnp.concatenate([zpad, p["gdn_dt_bias"]])[None, :], LANES)
    u, w, qd, ai, kdt, gl = _gdn_prepass(proj, ba, p["gdn_conv_w"], alog_row, dtb_row, seq,
                                         tiles["scan_pairs"])
    gdn = _gdn_scan(u, w, qd, ai, kdt, gl, proj, 3, p["gdn_norm_w"][None, :], batch, seq,
                    tiles["scan_pairs"])

    mixed = _merge(xb, attn, gdn, w_gs, w_gg, p["w_br_swa"].astype(BF16), p["w_br_gdn"].astype(BF16),
                   tiles["merge_tm"], tiles["merge_tn"])
    kv = _matmul(mem2d.astype(BF16), p["w_mem_kv"].astype(BF16), mem_len, tiles["kv_tn"], BF16)
    x2 = _mix_mem_ln(mixed, x2d, p["w_mix_o"].astype(BF16), p["ln1_g"][None, :], p["ln1_b"][None, :],
                     p["w_mem_q"].astype(BF16), kv, p["w_mem_o"].astype(BF16),
                     p["ln2_g"][None, :], p["ln2_b"][None, :], alpha, batch, seq, mem_len, tiles["mem_tm"])

    dff = p["w_down"].shape[0]
    tn = tiles["ffn_tn"]
    ffp = -(-dff // tn) * tn
    w_up = p["w_up"]
    cw = p["ffn_conv_w"]
    cb = p["ffn_conv_b"][None, :]
    x3 = _ffn_ln(x2,
                 _pad_cols(w_up[:, :dff], ffp).astype(BF16), _pad_cols(w_up[:, dff:], ffp).astype(BF16),
                 _pad_cols(cw[:, :dff], ffp), _pad_cols(cw[:, dff:], ffp),
                 _pad_cols(cb[:, :dff], ffp), _pad_cols(cb[:, dff:], ffp),
                 jnp.pad(p["w_down"], ((0, ffp - dff), (0, 0))).astype(BF16),
                 p["ln3_g"][None, :], p["ln3_b"][None, :], alpha, seq, tiles["ffn_tm"], tn)
    return x3


_TILES = dict(proj_tm=1024, proj_tn=1792, merge_tm=1024, merge_tn=1024, kv_tn=512,
              mem_tm=512, ffn_tm=1024, ffn_tn=512, swa_blocks=4, scan_pairs=4)

_PER_LAYER = ("w_in", "swa_sinks", "gdn_conv_w", "gdn_a_log", "gdn_dt_bias", "gdn_norm_w", "w_br_swa",
              "w_br_gdn", "w_mix_o", "ln1_g", "ln1_b", "w_mem_q", "w_mem_kv", "w_mem_o", "ln2_g", "ln2_b",
              "w_up", "ffn_conv_w", "ffn_conv_b", "w_down", "ln3_g", "ln3_b")


def _forward(x, mem, rel_bias, weights, tiles):
    batch, seq, d = x.shape
    mem_len = mem.shape[1]
    depth = weights["w_in"].shape[0]
    alpha = (2 * depth) ** 0.25
    bias_tab = _swa_bias_table(rel_bias)
    x2d = x.reshape(batch * seq, d)
    mem2d = mem.reshape(batch * mem_len, d)
    for l in range(depth):
        p = {name: weights[name][l] for name in _PER_LAYER}
        x2d = _layer(x2d, mem2d, batch, seq, mem_len, p, bias_tab, alpha, tiles)
    return x2d.reshape(batch, seq, d)


def kernel(x, mem, w_in, rel_bias, swa_sinks, gdn_conv_w, gdn_a_log, gdn_dt_bias, gdn_norm_w, w_br_swa, w_br_gdn, w_mix_o, ln1_g, ln1_b, w_mem_q, w_mem_kv, w_mem_o, ln2_g, ln2_b, w_up, ffn_conv_w, ffn_conv_b, w_down, ln3_g, ln3_b):
    weights = dict(w_in=w_in, swa_sinks=swa_sinks, gdn_conv_w=gdn_conv_w, gdn_a_log=gdn_a_log,
                   gdn_dt_bias=gdn_dt_bias, gdn_norm_w=gdn_norm_w, w_br_swa=w_br_swa, w_br_gdn=w_br_gdn,
                   w_mix_o=w_mix_o, ln1_g=ln1_g, ln1_b=ln1_b, w_mem_q=w_mem_q, w_mem_kv=w_mem_kv,
                   w_mem_o=w_mem_o, ln2_g=ln2_g, ln2_b=ln2_b, w_up=w_up, ffn_conv_w=ffn_conv_w,
                   ffn_conv_b=ffn_conv_b, w_down=w_down, ln3_g=ln3_g, ln3_b=ln3_b)
    return _forward(x, mem, rel_bias, weights, _TILES)
```

```python
import functools
import math

import jax
import jax.numpy as jnp
from jax import lax
from jax.experimental import pallas as pl
from jax.experimental.pallas import tpu as pltpu

F32 = jnp.float32
BF16 = jnp.bfloat16

LANES = 128
BF16_ROWS = 16
VMEM_LIMIT = 56 * 1024 * 1024

SWA_HEADS = 16
SWA_KV_HEADS = 2
SWA_HEAD_DIM = 64
SWA_BLOCK = 128
REL_BUCKETS = 32
REL_MAX_DIST = 128
GDN_HEADS = 8
GDN_HEAD_DIM = 128
GDN_CONV = 4
GDN_CHUNK = 64
GDN_PAIR = 2 * GDN_CHUNK
MEM_HEADS = 4
MEM_HEAD_DIM = 128
FFN_CONV = 3
NORM_EPS = 1e-5
NEG_INF = -1e30


def _params(*sem):
    return pltpu.CompilerParams(dimension_semantics=sem, vmem_limit_bytes=VMEM_LIMIT)


def _dot(a, b):
    return jnp.dot(a.astype(BF16), b.astype(BF16), preferred_element_type=F32)


def _dot_nt(a, b):
    return lax.dot_general(a.astype(BF16), b.astype(BF16), (((1,), (1,)), ((), ())),
                           preferred_element_type=F32)


def _split(a):
    hi = a.astype(BF16)
    lo = (a - hi.astype(F32)).astype(BF16)
    return hi, lo


def _dot3(a, b):
    ah, al = _split(a)
    bh, bl = _split(b)
    d = functools.partial(jnp.dot, preferred_element_type=F32)
    return d(ah, bh) + (d(al, bh) + d(ah, bl))


def _sigmoid(x):
    return 1.0 / (1.0 + jnp.exp(-x))


def _silu(x):
    return x * _sigmoid(x)


def _layer_norm(y, g, b):
    mu = jnp.mean(y, axis=-1, keepdims=True)
    yc = y - mu
    var = jnp.mean(yc * yc, axis=-1, keepdims=True)
    return yc * lax.rsqrt(var + NORM_EPS) * g + b


def _mm_kernel(x_ref, w_ref, o_ref):
    o_ref[...] = jnp.dot(x_ref[...], w_ref[...], preferred_element_type=F32).astype(o_ref.dtype)


def _matmul(x, w, tm, tn, out_dtype):
    m, k = x.shape
    n = w.shape[1]
    assert m % tm == 0 and n % tn == 0
    return pl.pallas_call(
        _mm_kernel,
        grid=(m // tm, n // tn),
        in_specs=[pl.BlockSpec((tm, k), lambda i, j: (i, 0)),
                  pl.BlockSpec((k, tn), lambda i, j: (0, j))],
        out_specs=pl.BlockSpec((tm, tn), lambda i, j: (i, j)),
        out_shape=jax.ShapeDtypeStruct((m, n), out_dtype),
        compiler_params=_params("parallel", "arbitrary"),
        name="matmul",
    )(x, w)


def _proj_kernel(x_ref, w_ref, ws_ref, o_ref, xb_ref, os_ref):
    @pl.when(pl.program_id(1) == 0)
    def _():
        xb_ref[...] = x_ref[...].astype(xb_ref.dtype)
        os_ref[...] = jnp.dot(xb_ref[...], ws_ref[...], preferred_element_type=F32)

    o_ref[...] = jnp.dot(xb_ref[...], w_ref[...], preferred_element_type=F32).astype(o_ref.dtype)


def _proj(x, w, w_small, tm, tn):
    m, k = x.shape
    n = w.shape[1]
    ns = w_small.shape[1]
    assert m % tm == 0 and n % tn == 0
    return pl.pallas_call(
        _proj_kernel,
        grid=(m // tm, n // tn),
        in_specs=[pl.BlockSpec((tm, k), lambda i, j: (i, 0)),
                  pl.BlockSpec((k, tn), lambda i, j: (0, j)),
                  pl.BlockSpec((k, ns), lambda i, j: (0, 0))],
        out_specs=[pl.BlockSpec((tm, tn), lambda i, j: (i, j)),
                   pl.BlockSpec((tm, k), lambda i, j: (i, 0)),
                   pl.BlockSpec((tm, ns), lambda i, j: (i, 0))],
        out_shape=[jax.ShapeDtypeStruct((m, n), BF16), jax.ShapeDtypeStruct((m, k), BF16),
                   jax.ShapeDtypeStruct((m, ns), F32)],
        compiler_params=_params("parallel", "arbitrary"),
        name="input_proj",
    )(x, w, w_small)


def _t5_causal_bucket(dist):
    max_exact = REL_BUCKETS // 2
    d = jnp.maximum(dist, 1).astype(F32)
    large = max_exact + (jnp.log(d / max_exact) / math.log(REL_MAX_DIST / max_exact)
                         * (REL_BUCKETS - max_exact)).astype(jnp.int32)
    large = jnp.minimum(large, REL_BUCKETS - 1)
    return jnp.where(dist < max_exact, dist, large)


def _swa_bias_table(rel_bias):
    nq, nk = SWA_BLOCK, 2 * SWA_BLOCK
    dist = nk - 1 - jnp.arange(nq + nk - 1)
    per_dist = rel_bias.astype(F32)[_t5_causal_bucket(jnp.maximum(dist, 0))]
    per_dist = jnp.where(((dist >= 0) & (dist < SWA_BLOCK))[:, None], per_dist, NEG_INF)
    u = jnp.pad(per_dist.T, ((0, 0), (0, 1)))
    skew = jnp.tile(u, (1, nq))[:, :nq * (nq + nk - 1)].reshape(-1, nq, nq + nk - 1)
    table = skew[:, :, nq - 1:]
    no_prev = jnp.where(jnp.arange(nk) < nq, NEG_INF, table)
    return jnp.stack([table, no_prev])


def _swa_kernel(sink_ref, q_ref, kv_ref, pkv_ref, bias0_ref, bias_ref, o_ref, *, blocks):
    half = SWA_HEAD_DIM
    kv = jnp.concatenate([pkv_ref[...], kv_ref[...]], axis=0)
    k = kv[:, :LANES]
    v = kv[:, LANES:]
    lo = lax.broadcasted_iota(jnp.int32, k.shape, 1) < half
    zero = jnp.zeros_like(k)

    def swap(t):
        return jnp.concatenate([t[:, half:], t[:, :half]], axis=1)

    def padded(t):
        tr = swap(t)
        return ((jnp.where(lo, t, zero), jnp.where(lo, zero, tr)),
                (jnp.where(lo, tr, zero), jnp.where(lo, zero, t)))

    kpad = padded(k)
    vpad = padded(v)
    out_lo = lax.broadcasted_iota(jnp.int32, (SWA_BLOCK, LANES), 1) < half
    group = SWA_HEADS // SWA_KV_HEADS
    for blk in range(blocks):
        rows = slice(blk * SWA_BLOCK, (blk + 1) * SWA_BLOCK)
        band = slice(blk * SWA_BLOCK, (blk + 2) * SWA_BLOCK)
        table_ref = bias0_ref if blk == 0 else bias_ref
        scores = [_dot_nt(q_ref[rows, (head // 2) * LANES:(head // 2 + 1) * LANES],
                          kpad[head // group][head % 2][band]) for head in range(SWA_HEADS)]
        es, inv = [], []
        for head in range(SWA_HEADS):
            s = scores[head] + table_ref[head]
            sink = sink_ref[head]
            m = jnp.maximum(jnp.max(s, axis=-1, keepdims=True), sink)
            e = jnp.exp(s - m)
            inv.append(1.0 / (jnp.sum(e, axis=-1, keepdims=True) + jnp.exp(sink - m)))
            es.append(e.astype(BF16))
        for p in range(SWA_HEADS // 2):
            acc = (_dot(es[2 * p], vpad[(2 * p) // group][0][band])
                   + _dot(es[2 * p + 1], vpad[(2 * p + 1) // group][1][band]))
            o_ref[rows, p * LANES:(p + 1) * LANES] = (
                acc * jnp.where(out_lo, inv[2 * p], inv[2 * p + 1])).astype(o_ref.dtype)


def _swa_attention(proj, q_col, kv_col, sinks, bias_tabs, seq, blocks):
    t = proj.shape[0]
    qw = SWA_HEADS * SWA_HEAD_DIM
    kvw = 2 * SWA_KV_HEADS * SWA_HEAD_DIM
    tq = blocks * SWA_BLOCK
    spb = seq // tq
    table = (None, SWA_HEADS, SWA_BLOCK, 2 * SWA_BLOCK)
    return pl.pallas_call(
        functools.partial(_swa_kernel, blocks=blocks),
        grid=(t // tq,),
        in_specs=[pl.BlockSpec(memory_space=pltpu.SMEM),
                  pl.BlockSpec((tq, qw), lambda i: (i, q_col)),
                  pl.BlockSpec((tq, kvw), lambda i: (i, kv_col)),
                  pl.BlockSpec((SWA_BLOCK, kvw), lambda i: (jnp.maximum(i * blocks - 1, 0), kv_col)),
                  pl.BlockSpec(table, lambda i: (jnp.where(i % spb == 0, 1, 0), 0, 0, 0)),
                  pl.BlockSpec(table, lambda i: (0, 0, 0, 0))],
        out_specs=pl.BlockSpec((tq, qw), lambda i: (i, 0)),
        out_shape=jax.ShapeDtypeStruct((t, qw), BF16),
        compiler_params=_params("parallel"),
        name="swa_attention",
    )(sinks, proj, proj, proj, bias_tabs, bias_tabs)


def _gdn_pre_kernel(q_ref, k_ref, v_ref, hq_ref, hk_ref, hv_ref, ba_ref, cw_ref, alog_ref, dtb_ref,
                    u_ref, w_ref, qd_ref, ai_ref, kdt_ref, gl_ref, *, first_step, pairs):
    n = GDN_PAIR
    c = GDN_CHUNK
    for pr in range(pairs):
        rows = pl.ds(pr * n, n)
        if pr == 0:
            halos, first = (hq_ref, hk_ref, hv_ref), first_step
        else:
            tail = pl.ds(pr * n - BF16_ROWS, BF16_ROWS)
            halos, first = (q_ref.at[tail], k_ref.at[tail], v_ref.at[tail]), False
        _gdn_pair(q_ref.at[rows], k_ref.at[rows], v_ref.at[rows], *halos, ba_ref.at[rows], cw_ref, alog_ref,
                  dtb_ref, u_ref.at[rows], w_ref.at[rows], qd_ref.at[rows], ai_ref.at[pl.ds(pr * c, c)],
                  kdt_ref.at[rows], gl_ref.at[pl.ds(pr, 1)], first)


def _gdn_pair(q_ref, k_ref, v_ref, hq_ref, hk_ref, hv_ref, ba_ref, cw_ref, alog_ref, dtb_ref,
              u_ref, w_ref, qd_ref, ai_ref, kdt_ref, gl_ref, first):
    n = GDN_PAIR
    c = GDN_CHUNK
    row = lax.broadcasted_iota(jnp.int32, (n, n), 0)
    col = lax.broadcasted_iota(jnp.int32, (n, n), 1)
    same = (row >= c) == (col >= c)
    tril = jnp.logical_and(same, row >= col)
    rowc = lax.broadcasted_iota(jnp.int32, (n, 1), 0)
    crow = lax.broadcasted_iota(jnp.int32, (c, n), 0)
    clane = lax.broadcasted_iota(jnp.int32, (c, n), 1)
    first_chunk = clane < c
    ccol = clane & (c - 1)
    ctril = crow >= ccol
    clower = crow > ccol
    cblk = crow ^ ccol
    ceye = jnp.where(crow == ccol, 1.0, 0.0).astype(F32)

    def compact(m):
        return jnp.where(first_chunk, m[:c], m[c:])

    def expand(m):
        z = jnp.zeros_like(m)
        return jnp.concatenate([jnp.where(first_chunk, m, z), jnp.where(first_chunk, z, m)], axis=0)

    ba = ba_ref[...]
    beta_all = _sigmoid(ba)
    x = ba + dtb_ref[...]
    softplus = jnp.maximum(x, 0.0) + jnp.log1p(jnp.exp(-jnp.abs(x)))
    g_all = -jnp.exp(alog_ref[...]) * softplus
    gc_all = _dot3(jnp.where(tril, 1.0, 0.0).astype(F32), g_all)
    gct_all = gc_all.T

    def conv_silu(cur_ref, halo_ref, which, sl):
        cur = cur_ref[:, sl].astype(F32)
        halo = jnp.where(first, 0.0, halo_ref[:, sl].astype(F32))
        ext = jnp.concatenate([halo, cur], axis=0)
        base = which * GDN_HEADS * GDN_HEAD_DIM
        cw = cw_ref[:, base + sl.start:base + sl.stop]
        y = cw[GDN_CONV - 1:GDN_CONV] * cur
        for j in range(1, GDN_CONV):
            y = y + cw[GDN_CONV - 1 - j:GDN_CONV - j] * pltpu.roll(ext, j, axis=0)[BF16_ROWS:]
        return _silu(y)

    def l2norm(t):
        return t * lax.rsqrt(jnp.sum(t * t, axis=-1, keepdims=True) + 1e-6)

    heads = range(GDN_HEADS)
    sls = [slice(h * LANES, (h + 1) * LANES) for h in heads]
    dotf = functools.partial(jnp.dot, preferred_element_type=F32)

    a_mats, rhs = [], []
    for h in heads:
        sl = sls[h]
        q = l2norm(conv_silu(q_ref, hq_ref, 0, sl)) * (GDN_HEAD_DIM ** -0.5)
        k = l2norm(conv_silu(k_ref, hk_ref, 1, sl))
        v = conv_silu(v_ref, hv_ref, 2, sl)
        beta = beta_all[:, h:h + 1]
        gc = gc_all[:, GDN_HEADS + h:GDN_HEADS + h + 1]
        gcr = gct_all[GDN_HEADS + h:GDN_HEADS + h + 1, :]
        kb = k * beta
        kq = _dot_nt(jnp.concatenate([kb, q], axis=0), k)
        gcc = jnp.where(first_chunk, gc[:c], gc[c:])
        decay = jnp.where(ctril, jnp.exp(jnp.where(ctril, gcc - gcr, 0.0)), 0.0)
        a_mats.append(jnp.where(clower, compact(kq[:n]) * decay, 0.0))
        egc = jnp.exp(gc)
        rhs.append(jnp.concatenate([v * beta, kb * egc], axis=1))
        gl_e = gc[c - 1:c, :]
        gl_o = gc[n - 1:n, :]
        kd = k * jnp.exp(jnp.where(rowc < c, gl_e, gl_o) - gc)
        qd_ref[:, sl] = (q * egc).astype(qd_ref.dtype)
        ai_ref[:, sl] = (compact(kq[n:]) * decay).astype(ai_ref.dtype)
        kdt_ref[:, sl] = kd.T.astype(kdt_ref.dtype)
        gl_ref[0, 0:1, sl] = jnp.broadcast_to(jnp.exp(gl_e), (1, LANES))
        gl_ref[0, 1:2, sl] = jnp.broadcast_to(jnp.exp(gl_o), (1, LANES))

    def dot3(a_split, b_split):
        (ah, al), (bh, bl) = a_split, b_split
        return dotf(jnp.concatenate([ah, al, ah], axis=1), jnp.concatenate([bh, bh, bl], axis=0))

    def expand_split(s):
        return tuple(expand(t) for t in s)

    base_blk = 8
    diag = jnp.logical_and(clower, cblk < base_blk)
    zero_bf = jnp.zeros((c, n), BF16)
    a_splits = [_split(a) for a in a_mats]
    d_splits = [tuple(jnp.where(diag, t, zero_bf) for t in s) for s in a_splits]
    d2s = [dot3(d, expand_split(d)) for d in d_splits]
    xs = []
    for h in heads:
        p = ceye - jnp.where(diag, a_mats[h], 0.0)
        d2 = _split(d2s[h])
        ps_ = _split(p)
        m = dot3(tuple(jnp.concatenate([s, t], axis=0) for s, t in zip(d2, ps_)), expand_split(d2))
        d2s[h] = m[:c]
        xs.append(p + m[c:])
    for h in heads:
        xs[h] = xs[h] + dot3(_split(xs[h]), expand_split(_split(d2s[h])))
    b = base_blk
    while b < c:
        off = jnp.logical_and(clower, jnp.logical_and(cblk >= b, cblk < 2 * b))
        o_splits = [tuple(jnp.where(off, t, zero_bf) for t in s) for s in a_splits]
        x_splits = [_split(x) for x in xs]
        ys = [dot3(o_splits[h], expand_split(x_splits[h])) for h in heads]
        for h in heads:
            xs[h] = xs[h] - dot3(x_splits[h], expand_split(_split(ys[h])))
        b *= 2
    zero_rhs = jnp.zeros((c, 2 * LANES), BF16)

    def chunk_diag(t):
        return jnp.concatenate([jnp.concatenate([t[:c], zero_rhs], axis=1),
                                jnp.concatenate([zero_rhs, t[c:]], axis=1)], axis=0)

    for h in heads:
        sol = dot3(_split(xs[h]), tuple(chunk_diag(t) for t in _split(rhs[h])))
        for ci in range(2):
            rs = slice(ci * c, (ci + 1) * c)
            u_ref[rs, sls[h]] = sol[:, (2 * ci) * LANES:(2 * ci + 1) * LANES]
            w_ref[rs, sls[h]] = sol[:, (2 * ci + 1) * LANES:(2 * ci + 2) * LANES].astype(w_ref.dtype)


def _gdn_scan_kernel(u_ref, w_ref, qd_ref, ai_ref, kdt_ref, gl_ref, z_ref, nw_ref, o_ref, st_ref):
    c = GDN_CHUNK

    @pl.when(pl.program_id(1) == 0)
    def _():
        st_ref[...] = jnp.zeros_like(st_ref)

    zeros = jnp.zeros((c, LANES), F32)
    nw = nw_ref[...]
    heads = range(GDN_HEADS)
    sls = [slice(h * LANES, (h + 1) * LANES) for h in heads]
    states = [st_ref[h] for h in heads]
    for ci in range(2 * gl_ref.shape[0]):
        pair, half = divmod(ci, 2)
        rs = slice(ci * c, (ci + 1) * c)
        ps = slice(pair * GDN_PAIR, (pair + 1) * GDN_PAIR)
        sbs = [s.astype(BF16) for s in states]
        ws = [_dot(w_ref[rs, sls[h]], sbs[h]) for h in heads]
        qs = [_dot(qd_ref[rs, sls[h]], sbs[h]) for h in heads]
        vpads = []
        for h in heads:
            vnew = u_ref[rs, sls[h]] - ws[h]
            vpads.append(jnp.concatenate([vnew, zeros] if half == 0 else [zeros, vnew], axis=0).astype(BF16))
        outs = [qs[h] + _dot(ai_ref[pair * c:(pair + 1) * c, sls[h]], vpads[h]) for h in heads]
        states = [states[h] * gl_ref[pair, half:half + 1, sls[h]] + _dot(kdt_ref[ps, sls[h]], vpads[h])
                  for h in heads]
        for h in heads:
            o = outs[h]
            z = z_ref[rs, sls[h]].astype(F32)
            o = o * lax.rsqrt(jnp.mean(o * o, axis=-1, keepdims=True) + 1e-6) * nw * _silu(z)
            o_ref[rs, sls[h]] = o.astype(o_ref.dtype)
    for h in heads:
        st_ref[h] = states[h]


def _gdn_kernel(q_ref, k_ref, v_ref, hq_ref, hk_ref, hv_ref, ba_ref, cw_ref, alog_ref, dtb_ref, z_ref, nw_ref,
                o_ref, u_ref, w_ref, qd_ref, ai_ref, kdt_ref, gl_ref, st_ref, *, pairs):
    _gdn_pre_kernel(q_ref, k_ref, v_ref, hq_ref, hk_ref, hv_ref, ba_ref, cw_ref, alog_ref, dtb_ref,
                    u_ref, w_ref, qd_ref, ai_ref, kdt_ref, gl_ref,
                    first_step=pl.program_id(1) == 0, pairs=pairs)
    _gdn_scan_kernel(u_ref, w_ref, qd_ref, ai_ref, kdt_ref, gl_ref, z_ref, nw_ref, o_ref, st_ref)


def _gated_deltanet(proj, ba, conv_w, alog_row, dtb_row, z_col, norm_w_row, batch, seq, pairs):
    t = proj.shape[0]
    gw = GDN_HEADS * GDN_HEAD_DIM
    n = pairs * GDN_PAIR
    spb = seq // n
    hb = n // BF16_ROWS

    def cur(cb):
        return pl.BlockSpec((n, gw), lambda b, i: (b * spb + i, cb))

    def halo(cb):
        return pl.BlockSpec((BF16_ROWS, gw), lambda b, i: (jnp.maximum((b * spb + i) * hb - 1, 0), cb))

    def const(shape):
        return pl.BlockSpec(shape, lambda b, i: (0, 0))

    return pl.pallas_call(
        functools.partial(_gdn_kernel, pairs=pairs),
        grid=(batch, spb),
        in_specs=[cur(0), cur(1), cur(2), halo(0), halo(1), halo(2),
                  pl.BlockSpec((n, LANES), lambda b, i: (b * spb + i, 0)),
                  const((GDN_CONV, 3 * gw)), const((1, LANES)), const((1, LANES)),
                  cur(z_col), const((1, LANES))],
        out_specs=pl.BlockSpec((n, gw), lambda b, i: (b * spb + i, 0)),
        out_shape=jax.ShapeDtypeStruct((t, gw), BF16),
        scratch_shapes=[pltpu.VMEM((n, gw), F32),
                        pltpu.VMEM((n, gw), BF16),
                        pltpu.VMEM((n, gw), BF16),
                        pltpu.VMEM((n // 2, gw), BF16),
                        pltpu.VMEM((n, gw), BF16),
                        pltpu.VMEM((pairs, 2, gw), F32),
                        pltpu.VMEM((GDN_HEADS, GDN_HEAD_DIM, GDN_HEAD_DIM), F32)],
        compiler_params=_params("arbitrary", "arbitrary"),
        name="gated_deltanet",
    )(proj, proj, proj, proj, proj, proj, ba, conv_w, alog_row, dtb_row, proj, norm_w_row)


def _merge_kernel(x_ref, a_ref, g_ref, wgs_ref, wgg_ref, wbs_ref, wbg_ref, o_ref):
    x = x_ref[...]
    gate_s = _sigmoid(jnp.dot(x, wgs_ref[...], preferred_element_type=F32))
    gate_g = _sigmoid(jnp.dot(x, wgg_ref[...], preferred_element_type=F32))
    y_s = jnp.dot(a_ref[...], wbs_ref[...], preferred_element_type=F32)
    y_g = jnp.dot(g_ref[...], wbg_ref[...], preferred_element_type=F32)
    o_ref[...] = (gate_s * y_s + gate_g * y_g).astype(o_ref.dtype)


def _merge(xb, attn, gdn, wgs, wgg, wbs, wbg, tm, tn):
    t, d = xb.shape
    ws = attn.shape[1]
    wg = gdn.shape[1]

    def wspec(k):
        return pl.BlockSpec((k, tn), lambda i, j: (0, j))

    return pl.pallas_call(
        _merge_kernel,
        grid=(t // tm, d // tn),
        in_specs=[pl.BlockSpec((tm, d), lambda i, j: (i, 0)),
                  pl.BlockSpec((tm, ws), lambda i, j: (i, 0)),
                  pl.BlockSpec((tm, wg), lambda i, j: (i, 0)),
                  wspec(d), wspec(d), wspec(ws), wspec(wg)],
        out_specs=pl.BlockSpec((tm, tn), lambda i, j: (i, j)),
        out_shape=jax.ShapeDtypeStruct((t, d), BF16),
        compiler_params=_params("parallel", "arbitrary"),
        name="branch_merge",
    )(xb, attn, gdn, wgs, wgg, wbs, wbg)


def _mix_mem_kernel(m_ref, x_ref, wmix_ref, g1_ref, b1_ref, wq_ref, kv_ref, wo_ref, g2_ref, b2_ref, o_ref,
                    *, alpha):
    y = alpha * x_ref[...] + jnp.dot(m_ref[...], wmix_ref[...], preferred_element_type=F32)
    x = _layer_norm(y, g1_ref[...], b1_ref[...])
    q = jnp.dot(x.astype(BF16), wq_ref[...], preferred_element_type=F32)
    mw = MEM_HEADS * MEM_HEAD_DIM
    outs = []
    qb = q.astype(BF16)
    scores = [_dot_nt(qb[:, h * MEM_HEAD_DIM:(h + 1) * MEM_HEAD_DIM],
                      kv_ref[:, h * MEM_HEAD_DIM:(h + 1) * MEM_HEAD_DIM]) for h in range(MEM_HEADS)]
    es, inv = [], []
    for h in range(MEM_HEADS):
        s = scores[h] * (MEM_HEAD_DIM ** -0.5)
        m = jnp.max(s, axis=-1, keepdims=True)
        e = jnp.exp(s - m)
        inv.append(1.0 / jnp.sum(e, axis=-1, keepdims=True))
        es.append(e.astype(BF16))
    for h in range(MEM_HEADS):
        vh = kv_ref[:, mw + h * MEM_HEAD_DIM:mw + (h + 1) * MEM_HEAD_DIM]
        outs.append(_dot(es[h], vh) * inv[h])
    o = jnp.concatenate(outs, axis=1)
    y = alpha * x + _dot(o, wo_ref[...])
    o_ref[...] = _layer_norm(y, g2_ref[...], b2_ref[...])


def _mix_mem_ln(mixed, x, wmix, g1, b1, wq, kv, wo, g2, b2, alpha, batch, seq, mem_len, tm):
    t, d = x.shape
    mw = MEM_HEADS * MEM_HEAD_DIM
    tpb = seq // tm
    vec = pl.BlockSpec((1, d), lambda bi, i: (0, 0))
    rows = pl.BlockSpec((tm, d), lambda bi, i: (bi * tpb + i, 0))

    def resident(shape):
        return pl.BlockSpec(shape, lambda bi, i: (0, 0), pipeline_mode=pl.Buffered(1))

    return pl.pallas_call(
        functools.partial(_mix_mem_kernel, alpha=alpha),
        grid=(batch, tpb),
        in_specs=[rows, rows, resident((d, d)), vec, vec,
                  resident((d, mw)),
                  pl.BlockSpec((mem_len, 2 * mw), lambda bi, i: (bi, 0)),
                  resident((mw, d)),
                  vec, vec],
        out_specs=rows,
        out_shape=jax.ShapeDtypeStruct((t, d), F32),
        compiler_params=_params("parallel", "arbitrary"),
        name="mixo_mem_attention_layernorm",
    )(mixed, x, wmix, g1, b1, wq, kv, wo, g2, b2)


def _ffn_kernel(x_ref, halo_ref, wg_ref, wu_ref, cg_ref, cu_ref, bg_ref, bu_ref, wd_ref, g_ref, b_ref,
                o_ref, xb_ref, *, alpha, tiles_per_batch):
    j = pl.program_id(1)
    hr = BF16_ROWS

    @pl.when(j == 0)
    def _():
        first = (pl.program_id(0) % tiles_per_batch) == 0
        xb_ref[:hr, :] = jnp.where(first, 0.0, halo_ref[...]).astype(BF16)
        xb_ref[hr:, :] = x_ref[...].astype(BF16)
        o_ref[...] = jnp.zeros_like(o_ref)

    xb = xb_ref[...]

    def conv(w_ref, cw_ref, bias_ref):
        h = jnp.dot(xb, w_ref[...], preferred_element_type=F32)
        cw = cw_ref[...]
        y = cw[FFN_CONV - 1:FFN_CONV] * h[hr:] + bias_ref[...]
        for k in range(1, FFN_CONV):
            y = y + cw[FFN_CONV - 1 - k:FFN_CONV - k] * pltpu.roll(h, k, axis=0)[hr:]
        return y

    act = _silu(conv(wg_ref, cg_ref, bg_ref)) * conv(wu_ref, cu_ref, bu_ref)
    o_ref[...] += _dot(act, wd_ref[...])

    @pl.when(j == pl.num_programs(1) - 1)
    def _():
        y = alpha * x_ref[...] + o_ref[...]
        o_ref[...] = _layer_norm(y, g_ref[...], b_ref[...])


def _ffn_ln(x, wg, wu, cg, cu, bg, bu, wd, g, b, alpha, seq, tm, tn):
    t, d = x.shape
    ff = wg.shape[1]
    tpb = seq // tm
    hb = tm // BF16_ROWS
    vec = pl.BlockSpec((1, d), lambda i, j: (0, 0))
    return pl.pallas_call(
        functools.partial(_ffn_kernel, alpha=alpha, tiles_per_batch=tpb),
        grid=(t // tm, ff // tn),
        in_specs=[pl.BlockSpec((tm, d), lambda i, j: (i, 0), pipeline_mode=pl.Buffered(1)),
                  pl.BlockSpec((BF16_ROWS, d), lambda i, j: (jnp.maximum(i * hb - 1, 0), 0)),
                  pl.BlockSpec((d, tn), lambda i, j: (0, j)),
                  pl.BlockSpec((d, tn), lambda i, j: (0, j)),
                  pl.BlockSpec((FFN_CONV, tn), lambda i, j: (0, j)),
                  pl.BlockSpec((FFN_CONV, tn), lambda i, j: (0, j)),
                  pl.BlockSpec((1, tn), lambda i, j: (0, j)),
                  pl.BlockSpec((1, tn), lambda i, j: (0, j)),
                  pl.BlockSpec((tn, d), lambda i, j: (j, 0)),
                  vec, vec],
        out_specs=pl.BlockSpec((tm, d), lambda i, j: (i, 0)),
        out_shape=jax.ShapeDtypeStruct((t, d), F32),
        scratch_shapes=[pltpu.VMEM((BF16_ROWS + tm, d), BF16)],
        compiler_params=_params("parallel", "arbitrary"),
        name="ffn_layernorm",
    )(x, x, wg, wu, cg, cu, bg, bu, wd, g, b)


def _pad_cols(a, n):
    return jnp.pad(a, ((0, 0), (0, n - a.shape[1])))


def _layer(x2d, mem2d, batch, seq, mem_len, p, bias_tab, alpha, tiles):
    d = x2d.shape[1]
    sq = SWA_HEADS * SWA_HEAD_DIM
    skv = SWA_KV_HEADS * SWA_HEAD_DIM
    gw = GDN_HEADS * GDN_HEAD_DIM
    w_in = p["w_in"]
    o_gdn = sq + 2 * skv
    o_ba = o_gdn + 4 * gw
    o_gate = o_ba + 2 * GDN_HEADS
    w_proj = jnp.concatenate([w_in[:, o_gdn:o_ba], w_in[:, :sq] * (SWA_HEAD_DIM ** -0.5),
                              w_in[:, sq:o_gdn]], axis=1).astype(BF16)
    w_ba = _pad_cols(w_in[:, o_ba:o_gate], LANES).astype(BF16)
    w_gs = w_in[:, o_gate:o_gate + d].astype(BF16)
    w_gg = w_in[:, o_gate + d:o_gate + 2 * d].astype(BF16)

    proj, xb, ba = _proj(x2d, w_proj, w_ba, tiles["proj_tm"], tiles["proj_tn"])

    attn = _swa_attention(proj, (4 * gw) // sq, (4 * gw + sq) // (2 * skv), p["swa_sinks"], bias_tab, seq,
                          tiles["swa_blocks"])

    zpad = jnp.zeros((GDN_HEADS,), F32)
    alog_row = _pad_cols(jnp.concatenate([zpad, p["gdn_a_log"]])[None, :], LANES)
    dtb_row = _pad_cols(jnp.concatenate([zpad, p["gdn_dt_bias"]])[None, :], LANES)
    gdn = _gated_deltanet(proj, ba, p["gdn_conv_w"], alog_row, dtb_row, 3, p["gdn_norm_w"][None, :],
                          batch, seq, tiles["scan_pairs"])

    mixed = _merge(xb, attn, gdn, w_gs, w_gg, p["w_br_swa"].astype(BF16), p["w_br_gdn"].astype(BF16),
                   tiles["merge_tm"], tiles["merge_tn"])
    kv = _matmul(mem2d.astype(BF16), p["w_mem_kv"].astype(BF16), mem_len, tiles["kv_tn"], BF16)
    x2 = _mix_mem_ln(mixed, x2d, p["w_mix_o"].astype(BF16), p["ln1_g"][None, :], p["ln1_b"][None, :],
                     p["w_mem_q"].astype(BF16), kv, p["w_mem_o"].astype(BF16),
                     p["ln2_g"][None, :], p["ln2_b"][None, :], alpha, batch, seq, mem_len, tiles["mem_tm"])

    dff = p["w_down"].shape[0]
    tn = tiles["ffn_tn"]
    ffp = -(-dff // tn) * tn
    w_up = p["w_up"]
    cw = p["ffn_conv_w"]
    cb = p["ffn_conv_b"][None, :]
    x3 = _ffn_ln(x2,
                 _pad_cols(w_up[:, :dff], ffp).astype(BF16), _pad_cols(w_up[:, dff:], ffp).astype(BF16),
                 _pad_cols(cw[:, :dff], ffp), _pad_cols(cw[:, dff:], ffp),
                 _pad_cols(cb[:, :dff], ffp), _pad_cols(cb[:, dff:], ffp),
                 jnp.pad(p["w_down"], ((0, ffp - dff), (0, 0))).astype(BF16),
                 p["ln3_g"][None, :], p["ln3_b"][None, :], alpha, seq, tiles["ffn_tm"], tn)
    return x3


_TILES = dict(proj_tm=1024, proj_tn=1792, merge_tm=1024, merge_tn=1024, kv_tn=512,
              mem_tm=512, ffn_tm=1024, ffn_tn=512, swa_blocks=4, scan_pairs=4)

_PER_LAYER = ("w_in", "swa_sinks", "gdn_conv_w", "gdn_a_log", "gdn_dt_bias", "gdn_norm_w", "w_br_swa",
              "w_br_gdn", "w_mix_o", "ln1_g", "ln1_b", "w_mem_q", "w_mem_kv", "w_mem_o", "ln2_g", "ln2_b",
              "w_up", "ffn_conv_w", "ffn_conv_b", "w_down", "ln3_g", "ln3_b")


def _forward(x, mem, rel_bias, weights, tiles):
    batch, seq, d = x.shape
    mem_len = mem.shape[1]
    depth = weights["w_in"].shape[0]
    alpha = (2 * depth) ** 0.25
    bias_tab = _swa_bias_table(rel_bias)
    x2d = x.reshape(batch * seq, d)
    mem2d = mem.reshape(batch * mem_len, d)
    for l in range(depth):
        p = {name: weights[name][l] for name in _PER_LAYER}
        x2d = _layer(x2d, mem2d, batch, seq, mem_len, p, bias_tab, alpha, tiles)
    return x2d.reshape(batch, seq, d)


def kernel(x, mem, w_in, rel_bias, swa_sinks, gdn_conv_w, gdn_a_log, gdn_dt_bias, gdn_norm_w, w_br_swa, w_br_gdn, w_mix_o, ln1_g, ln1_b, w_mem_q, w_mem_kv, w_mem_o, ln2_g, ln2_b, w_up, ffn_conv_w, ffn_conv_b, w_down, ln3_g, ln3_b):
    weights = dict(w_in=w_in, swa_sinks=swa_sinks, gdn_conv_w=gdn_conv_w, gdn_a_log=gdn_a_log,
                   gdn_dt_bias=gdn_dt_bias, gdn_norm_w=gdn_norm_w, w_br_swa=w_br_swa, w_br_gdn=w_br_gdn,
                   w_mix_o=w_mix_o, ln1_g=ln1_g, ln1_b=ln1_b, w_mem_q=w_mem_q, w_mem_kv=w_mem_kv,
                   w_mem_o=w_mem_o, ln2_g=ln2_g, ln2_b=ln2_b, w_up=w_up, ffn_conv_w=ffn_conv_w,
                   ffn_conv_b=ffn_conv_b, w_down=w_down, ln3_g=ln3_g, ln3_b=ln3_b)
    return _forward(x, mem, rel_bias, weights, _TILES)
```

```python
import functools
import math

import jax
import jax.numpy as jnp
from jax import lax
from jax.experimental import pallas as pl
from jax.experimental.pallas import tpu as pltpu

F32 = jnp.float32
BF16 = jnp.bfloat16

LANES = 128
BF16_ROWS = 16
VMEM_LIMIT = 56 * 1024 * 1024

SWA_HEADS = 16
SWA_KV_HEADS = 2
SWA_HEAD_DIM = 64
SWA_BLOCK = 128
REL_BUCKETS = 32
REL_MAX_DIST = 128
GDN_HEADS = 8
GDN_HEAD_DIM = 128
GDN_CONV = 4
GDN_CHUNK = 64
GDN_PAIR = 2 * GDN_CHUNK
MEM_HEADS = 4
MEM_HEAD_DIM = 128
FFN_CONV = 3
NORM_EPS = 1e-5
NEG_INF = -1e30


def _params(*sem):
    return pltpu.CompilerParams(dimension_semantics=sem, vmem_limit_bytes=VMEM_LIMIT)


def _dot(a, b):
    return jnp.dot(a.astype(BF16), b.astype(BF16), preferred_element_type=F32)


def _dot_nt(a, b):
    return lax.dot_general(a.astype(BF16), b.astype(BF16), (((1,), (1,)), ((), ())),
                           preferred_element_type=F32)


def _split(a):
    hi = a.astype(BF16)
    lo = (a - hi.astype(F32)).astype(BF16)
    return hi, lo


def _dot3(a, b):
    ah, al = _split(a)
    bh, bl = _split(b)
    d = functools.partial(jnp.dot, preferred_element_type=F32)
    return d(ah, bh) + (d(al, bh) + d(ah, bl))


def _sigmoid(x):
    return 1.0 / (1.0 + jnp.exp(-x))


def _silu(x):
    return x * _sigmoid(x)


def _layer_norm(y, g, b):
    mu = jnp.mean(y, axis=-1, keepdims=True)
    yc = y - mu
    var = jnp.mean(yc * yc, axis=-1, keepdims=True)
    return yc * lax.rsqrt(var + NORM_EPS) * g + b


def _mm_kernel(x_ref, w_ref, o_ref):
    o_ref[...] = jnp.dot(x_ref[...], w_ref[...], preferred_element_type=F32).astype(o_ref.dtype)


def _matmul(x, w, tm, tn, out_dtype):
    m, k = x.shape
    n = w.shape[1]
    assert m % tm == 0 and n % tn == 0
    return pl.pallas_call(
        _mm_kernel,
        grid=(m // tm, n // tn),
        in_specs=[pl.BlockSpec((tm, k), lambda i, j: (i, 0)),
                  pl.BlockSpec((k, tn), lambda i, j: (0, j))],
        out_specs=pl.BlockSpec((tm, tn), lambda i, j: (i, j)),
        out_shape=jax.ShapeDtypeStruct((m, n), out_dtype),
        compiler_params=_params("parallel", "arbitrary"),
        name="matmul",
    )(x, w)


def _proj_kernel(x_ref, w_ref, ws_ref, o_ref, xb_ref, os_ref):
    @pl.when(pl.program_id(1) == 0)
    def _():
        xb_ref[...] = x_ref[...].astype(xb_ref.dtype)
        os_ref[...] = jnp.dot(xb_ref[...], ws_ref[...], preferred_element_type=F32)

    o_ref[...] = jnp.dot(xb_ref[...], w_ref[...], preferred_element_type=F32).astype(o_ref.dtype)


def _proj(x, w, w_small, tm, tn):
    m, k = x.shape
    n = w.shape[1]
    ns = w_small.shape[1]
    assert m % tm == 0 and n % tn == 0
    return pl.pallas_call(
        _proj_kernel,
        grid=(m // tm, n // tn),
        in_specs=[pl.BlockSpec((tm, k), lambda i, j: (i, 0)),
                  pl.BlockSpec((k, tn), lambda i, j: (0, j)),
                  pl.BlockSpec((k, ns), lambda i, j: (0, 0))],
        out_specs=[pl.BlockSpec((tm, tn), lambda i, j: (i, j)),
                   pl.BlockSpec((tm, k), lambda i, j: (i, 0)),
                   pl.BlockSpec((tm, ns), lambda i, j: (i, 0))],
        out_shape=[jax.ShapeDtypeStruct((m, n), BF16), jax.ShapeDtypeStruct((m, k), BF16),
                   jax.ShapeDtypeStruct((m, ns), F32)],
        compiler_params=_params("parallel", "arbitrary"),
        name="input_proj",
    )(x, w, w_small)


def _t5_causal_bucket(dist):
    max_exact = REL_BUCKETS // 2
    d = jnp.maximum(dist, 1).astype(F32)
    large = max_exact + (jnp.log(d / max_exact) / math.log(REL_MAX_DIST / max_exact)
                         * (REL_BUCKETS - max_exact)).astype(jnp.int32)
    large = jnp.minimum(large, REL_BUCKETS - 1)
    return jnp.where(dist < max_exact, dist, large)


def _swa_bias_table(rel_bias):
    nq, nk = SWA_BLOCK, 2 * SWA_BLOCK
    dist = nk - 1 - jnp.arange(nq + nk - 1)
    per_dist = rel_bias.astype(F32)[_t5_causal_bucket(jnp.maximum(dist, 0))]
    per_dist = jnp.where(((dist >= 0) & (dist < SWA_BLOCK))[:, None], per_dist, NEG_INF)
    u = jnp.pad(per_dist.T, ((0, 0), (0, 1)))
    skew = jnp.tile(u, (1, nq))[:, :nq * (nq + nk - 1)].reshape(-1, nq, nq + nk - 1)
    table = skew[:, :, nq - 1:]
    no_prev = jnp.where(jnp.arange(nk) < nq, NEG_INF, table)
    return jnp.stack([table, no_prev])


def _swa_kernel(sink_ref, q_ref, kv_ref, pkv_ref, bias0_ref, bias_ref, o_ref, *, blocks):
    half = SWA_HEAD_DIM
    kv = jnp.concatenate([pkv_ref[...], kv_ref[...]], axis=0)
    k = kv[:, :LANES]
    v = kv[:, LANES:]
    lo = lax.broadcasted_iota(jnp.int32, k.shape, 1) < half
    zero = jnp.zeros_like(k)

    def swap(t):
        return jnp.concatenate([t[:, half:], t[:, :half]], axis=1)

    def padded(t):
        tr = swap(t)
        return ((jnp.where(lo, t, zero), jnp.where(lo, zero, tr)),
                (jnp.where(lo, tr, zero), jnp.where(lo, zero, t)))

    kpad = padded(k)
    vpad = padded(v)
    out_lo = lax.broadcasted_iota(jnp.int32, (SWA_BLOCK, LANES), 1) < half
    group = SWA_HEADS // SWA_KV_HEADS
    for blk in range(blocks):
        rows = slice(blk * SWA_BLOCK, (blk + 1) * SWA_BLOCK)
        band = slice(blk * SWA_BLOCK, (blk + 2) * SWA_BLOCK)
        table_ref = bias0_ref if blk == 0 else bias_ref
        scores = [_dot_nt(q_ref[rows, (head // 2) * LANES:(head // 2 + 1) * LANES],
                          kpad[head // group][head % 2][band]) for head in range(SWA_HEADS)]
        es, inv = [], []
        for head in range(SWA_HEADS):
            s = scores[head] + table_ref[head]
            sink = sink_ref[head]
            m = jnp.maximum(jnp.max(s, axis=-1, keepdims=True), sink)
            e = jnp.exp(s - m)
            inv.append(1.0 / (jnp.sum(e, axis=-1, keepdims=True) + jnp.exp(sink - m)))
            es.append(e.astype(BF16))
        for p in range(SWA_HEADS // 2):
            acc = (_dot(es[2 * p], vpad[(2 * p) // group][0][band])
                   + _dot(es[2 * p + 1], vpad[(2 * p + 1) // group][1][band]))
            o_ref[rows, p * LANES:(p + 1) * LANES] = (
                acc * jnp.where(out_lo, inv[2 * p], inv[2 * p + 1])).astype(o_ref.dtype)


def _swa_attention(proj, q_col, kv_col, sinks, bias_tabs, seq, blocks):
    t = proj.shape[0]
    qw = SWA_HEADS * SWA_HEAD_DIM
    kvw = 2 * SWA_KV_HEADS * SWA_HEAD_DIM
    tq = blocks * SWA_BLOCK
    spb = seq // tq
    table = (None, SWA_HEADS, SWA_BLOCK, 2 * SWA_BLOCK)
    return pl.pallas_call(
        functools.partial(_swa_kernel, blocks=blocks),
        grid=(t // tq,),
        in_specs=[pl.BlockSpec(memory_space=pltpu.SMEM),
                  pl.BlockSpec((tq, qw), lambda i: (i, q_col)),
                  pl.BlockSpec((tq, kvw), lambda i: (i, kv_col)),
                  pl.BlockSpec((SWA_BLOCK, kvw), lambda i: (jnp.maximum(i * blocks - 1, 0), kv_col)),
                  pl.BlockSpec(table, lambda i: (jnp.where(i % spb == 0, 1, 0), 0, 0, 0)),
                  pl.BlockSpec(table, lambda i: (0, 0, 0, 0))],
        out_specs=pl.BlockSpec((tq, qw), lambda i: (i, 0)),
        out_shape=jax.ShapeDtypeStruct((t, qw), BF16),
        compiler_params=_params("parallel"),
        name="swa_attention",
    )(sinks, proj, proj, proj, bias_tabs, bias_tabs)


def _gdn_pre_kernel(q_ref, k_ref, v_ref, hq_ref, hk_ref, hv_ref, ba_ref, cw_ref, alog_ref, dtb_ref,
                    u_ref, w_ref, qd_ref, ai_ref, kdt_ref, gl_ref, *, first_step, pairs):
    n = GDN_PAIR
    c = GDN_CHUNK
    for pr in range(pairs):
        rows = pl.ds(pr * n, n)
        if pr == 0:
            halos, first = (hq_ref, hk_ref, hv_ref), first_step
        else:
            tail = pl.ds(pr * n - BF16_ROWS, BF16_ROWS)
            halos, first = (q_ref.at[tail], k_ref.at[tail], v_ref.at[tail]), False
        _gdn_pair(q_ref.at[rows], k_ref.at[rows], v_ref.at[rows], *halos, ba_ref.at[rows], cw_ref, alog_ref,
                  dtb_ref, u_ref.at[rows], w_ref.at[rows], qd_ref.at[rows], ai_ref.at[pl.ds(pr * c, c)],
                  kdt_ref.at[rows], gl_ref.at[pl.ds(pr, 1)], first)


def _gdn_pair(q_ref, k_ref, v_ref, hq_ref, hk_ref, hv_ref, ba_ref, cw_ref, alog_ref, dtb_ref,
              u_ref, w_ref, qd_ref, ai_ref, kdt_ref, gl_ref, first):
    n = GDN_PAIR
    c = GDN_CHUNK
    row = lax.broadcasted_iota(jnp.int32, (n, n), 0)
    col = lax.broadcasted_iota(jnp.int32, (n, n), 1)
    same = (row >= c) == (col >= c)
    tril = jnp.logical_and(same, row >= col)
    rowc = lax.broadcasted_iota(jnp.int32, (n, 1), 0)
    crow = lax.broadcasted_iota(jnp.int32, (c, n), 0)
    clane = lax.broadcasted_iota(jnp.int32, (c, n), 1)
    first_chunk = clane < c
    ccol = clane & (c - 1)
    ctril = crow >= ccol
    clower = crow > ccol
    cblk = crow ^ ccol
    ceye = jnp.where(crow == ccol, 1.0, 0.0).astype(F32)

    def compact(m):
        return jnp.where(first_chunk, m[:c], m[c:])

    def expand(m):
        z = jnp.zeros_like(m)
        return jnp.concatenate([jnp.where(first_chunk, m, z), jnp.where(first_chunk, z, m)], axis=0)

    ba = ba_ref[...]
    beta_all = _sigmoid(ba)
    x = ba + dtb_ref[...]
    softplus = jnp.maximum(x, 0.0) + jnp.log1p(jnp.exp(-jnp.abs(x)))
    g_all = -jnp.exp(alog_ref[...]) * softplus
    gc_all = _dot3(jnp.where(tril, 1.0, 0.0).astype(F32), g_all)
    gct_all = gc_all.T

    def conv_silu(cur_ref, halo_ref, which, sl):
        cur = cur_ref[:, sl].astype(F32)
        halo = jnp.where(first, 0.0, halo_ref[:, sl].astype(F32))
        ext = jnp.concatenate([halo, cur], axis=0)
        base = which * GDN_HEADS * GDN_HEAD_DIM
        cw = cw_ref[:, base + sl.start:base + sl.stop]
        y = cw[GDN_CONV - 1:GDN_CONV] * cur
        for j in range(1, GDN_CONV):
            y = y + cw[GDN_CONV - 1 - j:GDN_CONV - j] * pltpu.roll(ext, j, axis=0)[BF16_ROWS:]
        return _silu(y)

    def l2norm(t):
        return t * lax.rsqrt(jnp.sum(t * t, axis=-1, keepdims=True) + 1e-6)

    heads = range(GDN_HEADS)
    sls = [slice(h * LANES, (h + 1) * LANES) for h in heads]
    dotf = functools.partial(jnp.dot, preferred_element_type=F32)

    a_mats, rhs = [], []
    for h in heads:
        sl = sls[h]
        q = l2norm(conv_silu(q_ref, hq_ref, 0, sl)) * (GDN_HEAD_DIM ** -0.5)
        k = l2norm(conv_silu(k_ref, hk_ref, 1, sl))
        v = conv_silu(v_ref, hv_ref, 2, sl)
        beta = beta_all[:, h:h + 1]
        gc = gc_all[:, GDN_HEADS + h:GDN_HEADS + h + 1]
        gcr = gct_all[GDN_HEADS + h:GDN_HEADS + h + 1, :]
        kb = k * beta
        kq = _dot_nt(jnp.concatenate([kb, q], axis=0), k)
        gcc = jnp.where(first_chunk, gc[:c], gc[c:])
        decay = jnp.where(ctril, jnp.exp(jnp.where(ctril, gcc - gcr, 0.0)), 0.0)
        a_mats.append(jnp.where(clower, compact(kq[:n]) * decay, 0.0))
        egc = jnp.exp(gc)
        rhs.append(jnp.concatenate([v * beta, kb * egc], axis=1))
        gl_e = gc[c - 1:c, :]
        gl_o = gc[n - 1:n, :]
        kd = k * jnp.exp(jnp.where(rowc < c, gl_e, gl_o) - gc)
        qd_ref[:, sl] = (q * egc).astype(qd_ref.dtype)
        ai_ref[:, sl] = (compact(kq[n:]) * decay).astype(ai_ref.dtype)
        kdt_ref[:, sl] = kd.T.astype(kdt_ref.dtype)
        gl_ref[0, 0:1, sl] = jnp.broadcast_to(jnp.exp(gl_e), (1, LANES))
        gl_ref[0, 1:2, sl] = jnp.broadcast_to(jnp.exp(gl_o), (1, LANES))

    def dot3(a_split, b_split):
        (ah, al), (bh, bl) = a_split, b_split
        return dotf(jnp.concatenate([ah, al, ah], axis=1), jnp.concatenate([bh, bh, bl], axis=0))

    def expand_split(s):
        return tuple(expand(t) for t in s)

    base_blk = 8
    diag = jnp.logical_and(clower, cblk < base_blk)
    zero_bf = jnp.zeros((c, n), BF16)
    a_splits = [_split(a) for a in a_mats]
    d_splits = [tuple(jnp.where(diag, t, zero_bf) for t in s) for s in a_splits]
    d2s = [dot3(d, expand_split(d)) for d in d_splits]
    xs = []
    for h in heads:
        p = ceye - jnp.where(diag, a_mats[h], 0.0)
        d2 = _split(d2s[h])
        ps_ = _split(p)
        m = dot3(tuple(jnp.concatenate([s, t], axis=0) for s, t in zip(d2, ps_)), expand_split(d2))
        d2s[h] = m[:c]
        xs.append(p + m[c:])
    for h in heads:
        xs[h] = xs[h] + dot3(_split(xs[h]), expand_split(_split(d2s[h])))
    b = base_blk
    while b < c:
        off = jnp.logical_and(clower, jnp.logical_and(cblk >= b, cblk < 2 * b))
        o_splits = [tuple(jnp.where(off, t, zero_bf) for t in s) for s in a_splits]
        x_splits = [_split(x) for x in xs]
        ys = [dot3(o_splits[h], expand_split(x_splits[h])) for h in heads]
        for h in heads:
            xs[h] = xs[h] - dot3(x_splits[h], expand_split(_split(ys[h])))
        b *= 2
    zero_rhs = jnp.zeros((c, 2 * LANES), BF16)

    def chunk_diag(t):
        return jnp.concatenate([jnp.concatenate([t[:c], zero_rhs], axis=1),
                                jnp.concatenate([zero_rhs, t[c:]], axis=1)], axis=0)

    for h in heads:
        sol = dot3(_split(xs[h]), tuple(chunk_diag(t) for t in _split(rhs[h])))
        for ci in range(2):
            rs = slice(ci * c, (ci + 1) * c)
            u_ref[rs, sls[h]] = sol[:, (2 * ci) * LANES:(2 * ci + 1) * LANES]
            w_ref[rs, sls[h]] = sol[:, (2 * ci + 1) * LANES:(2 * ci + 2) * LANES].astype(w_ref.dtype)


def _gdn_scan_kernel(u_ref, w_ref, qd_ref, ai_ref, kdt_ref, gl_ref, z_ref, nw_ref, o_ref, st_ref):
    c = GDN_CHUNK

    @pl.when(pl.program_id(1) == 0)
    def _():
        st_ref[...] = jnp.zeros_like(st_ref)

    zeros = jnp.zeros((c, LANES), F32)
    nw = nw_ref[...]
    heads = range(GDN_HEADS)
    sls = [slice(h * LANES, (h + 1) * LANES) for h in heads]
    states = [st_ref[h] for h in heads]
    for ci in range(2 * gl_ref.shape[0]):
        pair, half = divmod(ci, 2)
        rs = slice(ci * c, (ci + 1) * c)
        ps = slice(pair * GDN_PAIR, (pair + 1) * GDN_PAIR)
        sbs = [s.astype(BF16) for s in states]
        ws = [_dot(w_ref[rs, sls[h]], sbs[h]) for h in heads]
        qs = [_dot(qd_ref[rs, sls[h]], sbs[h]) for h in heads]
        vpads = []
        for h in heads:
            vnew = u_ref[rs, sls[h]] - ws[h]
            vpads.append(jnp.concatenate([vnew, zeros] if half == 0 else [zeros, vnew], axis=0).astype(BF16))
        outs = [qs[h] + _dot(ai_ref[pair * c:(pair + 1) * c, sls[h]], vpads[h]) for h in heads]
        states = [states[h] * gl_ref[pair, half:half + 1, sls[h]] + _dot(kdt_ref[ps, sls[h]], vpads[h])
                  for h in heads]
        for h in heads:
            o = outs[h]
            z = z_ref[rs, sls[h]].astype(F32)
            o = o * lax.rsqrt(jnp.mean(o * o, axis=-1, keepdims=True) + 1e-6) * nw * _silu(z)
            o_ref[rs, sls[h]] = o.astype(o_ref.dtype)
    for h in heads:
        st_ref[h] = states[h]


def _gdn_kernel(q_ref, k_ref, v_ref, hq_ref, hk_ref, hv_ref, ba_ref, cw_ref, alog_ref, dtb_ref, z_ref, nw_ref,
                o_ref, u_ref, w_ref, qd_ref, ai_ref, kdt_ref, gl_ref, st_ref, *, pairs):
    _gdn_pre_kernel(q_ref, k_ref, v_ref, hq_ref, hk_ref, hv_ref, ba_ref, cw_ref, alog_ref, dtb_ref,
                    u_ref, w_ref, qd_ref, ai_ref, kdt_ref, gl_ref,
                    first_step=pl.program_id(1) == 0, pairs=pairs)
    _gdn_scan_kernel(u_ref, w_ref, qd_ref, ai_ref, kdt_ref, gl_ref, z_ref, nw_ref, o_ref, st_ref)


def _gated_deltanet(proj, ba, conv_w, alog_row, dtb_row, z_col, norm_w_row, batch, seq, pairs):
    t = proj.shape[0]
    gw = GDN_HEADS * GDN_HEAD_DIM
    n = pairs * GDN_PAIR
    spb = seq // n
    hb = n // BF16_ROWS

    def cur(cb):
        return pl.BlockSpec((n, gw), lambda b, i: (b * spb + i, cb))

    def halo(cb):
        return pl.BlockSpec((BF16_ROWS, gw), lambda b, i: (jnp.maximum((b * spb + i) * hb - 1, 0), cb))

    def const(shape):
        return pl.BlockSpec(shape, lambda b, i: (0, 0))

    return pl.pallas_call(
        functools.partial(_gdn_kernel, pairs=pairs),
        grid=(batch, spb),
        in_specs=[cur(0), cur(1), cur(2), halo(0), halo(1), halo(2),
                  pl.BlockSpec((n, LANES), lambda b, i: (b * spb + i, 0)),
                  const((GDN_CONV, 3 * gw)), const((1, LANES)), const((1, LANES)),
                  cur(z_col), const((1, LANES))],
        out_specs=pl.BlockSpec((n, gw), lambda b, i: (b * spb + i, 0)),
        out_shape=jax.ShapeDtypeStruct((t, gw), BF16),
        scratch_shapes=[pltpu.VMEM((n, gw), F32),
                        pltpu.VMEM((n, gw), BF16),
                        pltpu.VMEM((n, gw), BF16),
                        pltpu.VMEM((n // 2, gw), BF16),
                        pltpu.VMEM((n, gw), BF16),
                        pltpu.VMEM((pairs, 2, gw), F32),
                        pltpu.VMEM((GDN_HEADS, GDN_HEAD_DIM, GDN_HEAD_DIM), F32)],
        compiler_params=_params("arbitrary", "arbitrary"),
        name="gated_deltanet",
    )(proj, proj, proj, proj, proj, proj, ba, conv_w, alog_row, dtb_row, proj, norm_w_row)


def _merge_kernel(x_ref, a_ref, g_ref, wgs_ref, wgg_ref, wbs_ref, wbg_ref, o_ref):
    x = x_ref[...]
    gate_s = _sigmoid(jnp.dot(x, wgs_ref[...], preferred_element_type=F32))
    gate_g = _sigmoid(jnp.dot(x, wgg_ref[...], preferred_element_type=F32))
    y_s = jnp.dot(a_ref[...], wbs_ref[...], preferred_element_type=F32)
    y_g = jnp.dot(g_ref[...], wbg_ref[...], preferred_element_type=F32)
    o_ref[...] = (gate_s * y_s + gate_g * y_g).astype(o_ref.dtype)


def _merge(xb, attn, gdn, wgs, wgg, wbs, wbg, tm, tn):
    t, d = xb.shape
    ws = attn.shape[1]
    wg = gdn.shape[1]

    def wspec(k):
        return pl.BlockSpec((k, tn), lambda i, j: (0, j))

    return pl.pallas_call(
        _merge_kernel,
        grid=(t // tm, d // tn),
        in_specs=[pl.BlockSpec((tm, d), lambda i, j: (i, 0)),
                  pl.BlockSpec((tm, ws), lambda i, j: (i, 0)),
                  pl.BlockSpec((tm, wg), lambda i, j: (i, 0)),
                  wspec(d), wspec(d), wspec(ws), wspec(wg)],
        out_specs=pl.BlockSpec((tm, tn), lambda i, j: (i, j)),
        out_shape=jax.ShapeDtypeStruct((t, d), BF16),
        compiler_params=_params("parallel", "arbitrary"),
        name="branch_merge",
    )(xb, attn, gdn, wgs, wgg, wbs, wbg)


def _mix_mem_kernel(m_ref, x_ref, wmix_ref, g1_ref, b1_ref, wq_ref, kv_ref, wo_ref, g2_ref, b2_ref, o_ref,
                    *, alpha):
    y = alpha * x_ref[...] + jnp.dot(m_ref[...], wmix_ref[...], preferred_element_type=F32)
    x = _layer_norm(y, g1_ref[...], b1_ref[...])
    q = jnp.dot(x.astype(BF16), wq_ref[...], preferred_element_type=F32)
    mw = MEM_HEADS * MEM_HEAD_DIM
    outs = []
    qb = q.astype(BF16)
    scores = [_dot_nt(qb[:, h * MEM_HEAD_DIM:(h + 1) * MEM_HEAD_DIM],
                      kv_ref[:, h * MEM_HEAD_DIM:(h + 1) * MEM_HEAD_DIM]) for h in range(MEM_HEADS)]
    es, inv = [], []
    for h in range(MEM_HEADS):
        s = scores[h] * (MEM_HEAD_DIM ** -0.5)
        m = jnp.max(s, axis=-1, keepdims=True)
        e = jnp.exp(s - m)
        inv.append(1.0 / jnp.sum(e, axis=-1, keepdims=True))
        es.append(e.astype(BF16))
    for h in range(MEM_HEADS):
        vh = kv_ref[:, mw + h * MEM_HEAD_DIM:mw + (h + 1) * MEM_HEAD_DIM]
        outs.append(_dot(es[h], vh) * inv[h])
    o = jnp.concatenate(outs, axis=1)
    y = alpha * x + _dot(o, wo_ref[...])
    o_ref[...] = _layer_norm(y, g2_ref[...], b2_ref[...])


def _mix_mem_ln(mixed, x, wmix, g1, b1, wq, kv, wo, g2, b2, alpha, batch, seq, mem_len, tm):
    t, d = x.shape
    mw = MEM_HEADS * MEM_HEAD_DIM
    tpb = seq // tm
    vec = pl.BlockSpec((1, d), lambda bi, i: (0, 0))
    rows = pl.BlockSpec((tm, d), lambda bi, i: (bi * tpb + i, 0))

    def resident(shape):
        return pl.BlockSpec(shape, lambda bi, i: (0, 0), pipeline_mode=pl.Buffered(1))

    return pl.pallas_call(
        functools.partial(_mix_mem_kernel, alpha=alpha),
        grid=(batch, tpb),
        in_specs=[rows, rows, resident((d, d)), vec, vec,
                  resident((d, mw)),
                  pl.BlockSpec((mem_len, 2 * mw), lambda bi, i: (bi, 0)),
                  resident((mw, d)),
                  vec, vec],
        out_specs=rows,
        out_shape=jax.ShapeDtypeStruct((t, d), F32),
        compiler_params=_params("parallel", "arbitrary"),
        name="mixo_mem_attention_layernorm",
    )(mixed, x, wmix, g1, b1, wq, kv, wo, g2, b2)


def _ffn_kernel(x_ref, halo_ref, wg_ref, wu_ref, cg_ref, cu_ref, bg_ref, bu_ref, wd_ref, g_ref, b_ref,
                o_ref, xb_ref, *, alpha, tiles_per_batch):
    j = pl.program_id(1)
    hr = BF16_ROWS

    @pl.when(j == 0)
    def _():
        first = (pl.program_id(0) % tiles_per_batch) == 0
        xb_ref[:hr, :] = jnp.where(first, 0.0, halo_ref[...]).astype(BF16)
        xb_ref[hr:, :] = x_ref[...].astype(BF16)
        o_ref[...] = jnp.zeros_like(o_ref)

    xb = xb_ref[...]

    def conv(w_ref, cw_ref, bias_ref):
        h = jnp.dot(xb, w_ref[...], preferred_element_type=F32)
        cw = cw_ref[...]
        y = cw[FFN_CONV - 1:FFN_CONV] * h[hr:] + bias_ref[...]
        for k in range(1, FFN_CONV):
            y = y + cw[FFN_CONV - 1 - k:FFN_CONV - k] * pltpu.roll(h, k, axis=0)[hr:]
        return y

    act = _silu(conv(wg_ref, cg_ref, bg_ref)) * conv(wu_ref, cu_ref, bu_ref)
    o_ref[...] += _dot(act, wd_ref[...])

    @pl.when(j == pl.num_programs(1) - 1)
    def _():
        y = alpha * x_ref[...] + o_ref[...]
        o_ref[...] = _layer_norm(y, g_ref[...], b_ref[...])


def _ffn_ln(x, wg, wu, cg, cu, bg, bu, wd, g, b, alpha, seq, tm, tn):
    t, d = x.shape
    ff = wg.shape[1]
    tpb = seq // tm
    hb = tm // BF16_ROWS
    vec = pl.BlockSpec((1, d), lambda i, j: (0, 0))
    return pl.pallas_call(
        functools.partial(_ffn_kernel, alpha=alpha, tiles_per_batch=tpb),
        grid=(t // tm, ff // tn),
        in_specs=[pl.BlockSpec((tm, d), lambda i, j: (i, 0), pipeline_mode=pl.Buffered(1)),
                  pl.BlockSpec((BF16_ROWS, d), lambda i, j: (jnp.maximum(i * hb - 1, 0), 0)),
                  pl.BlockSpec((d, tn), lambda i, j: (0, j)),
                  pl.BlockSpec((d, tn), lambda i, j: (0, j)),
                  pl.BlockSpec((FFN_CONV, tn), lambda i, j: (0, j)),
                  pl.BlockSpec((FFN_CONV, tn), lambda i, j: (0, j)),
                  pl.BlockSpec((1, tn), lambda i, j: (0, j)),
                  pl.BlockSpec((1, tn), lambda i, j: (0, j)),
                  pl.BlockSpec((tn, d), lambda i, j: (j, 0)),
                  vec, vec],
        out_specs=pl.BlockSpec((tm, d), lambda i, j: (i, 0)),
        out_shape=jax.ShapeDtypeStruct((t, d), F32),
        scratch_shapes=[pltpu.VMEM((BF16_ROWS + tm, d), BF16)],
        compiler_params=_params("parallel", "arbitrary"),
        name="ffn_layernorm",
    )(x, x, wg, wu, cg, cu, bg, bu, wd, g, b)


def _pad_cols(a, n):
    return jnp.pad(a, ((0, 0), (0, n - a.shape[1])))


def _layer(x2d, mem2d, batch, seq, mem_len, p, bias_tab, alpha, tiles):
    d = x2d.shape[1]
    sq = SWA_HEADS * SWA_HEAD_DIM
    skv = SWA_KV_HEADS * SWA_HEAD_DIM
    gw = GDN_HEADS * GDN_HEAD_DIM
    w_in = p["w_in"]
    o_gdn = sq + 2 * skv
    o_ba = o_gdn + 4 * gw
    o_gate = o_ba + 2 * GDN_HEADS
    w_proj = jnp.concatenate([w_in[:, o_gdn:o_ba], w_in[:, :sq] * (SWA_HEAD_DIM ** -0.5),
                              w_in[:, sq:o_gdn]], axis=1).astype(BF16)
    w_ba = _pad_cols(w_in[:, o_ba:o_gate], LANES).astype(BF16)
    w_gs = w_in[:, o_gate:o_gate + d].astype(BF16)
    w_gg = w_in[:, o_gate + d:o_gate + 2 * d].astype(BF16)

    proj, xb, ba = _proj(x2d, w_proj, w_ba, tiles["proj_tm"], tiles["proj_tn"])

    attn = _swa_attention(proj, (4 * gw) // sq, (4 * gw + sq) // (2 * skv), p["swa_sinks"], bias_tab, seq,
                          tiles["swa_blocks"])

    zpad = jnp.zeros((GDN_HEADS,), F32)
    alog_row = _pad_cols(jnp.concatenate([zpad, p["gdn_a_log"]])[None, :], LANES)
    dtb_row = _pad_cols(jnp.concatenate([zpad, p["gdn_dt_bias"]])[None, :], LANES)
    gdn = _gated_deltanet(proj, ba, p["gdn_conv_w"], alog_row, dtb_row, 3, p["gdn_norm_w"][None, :],
                          batch, seq, tiles["scan_pairs"])

    mixed = _merge(xb, attn, gdn, w_gs, w_gg, p["w_br_swa"].astype(BF16), p["w_br_gdn"].astype(BF16),
                   tiles["merge_tm"], tiles["merge_tn"])
    kv = _matmul(mem2d.astype(BF16), p["w_mem_kv"].astype(BF16), mem_len, tiles["kv_tn"], BF16)
    x2 = _mix_mem_ln(mixed, x2d, p["w_mix_o"].astype(BF16), p["ln1_g"][None, :], p["ln1_b"][None, :],
                     p["w_mem_q"].astype(BF16), kv, p["w_mem_o"].astype(BF16),
                     p["ln2_g"][None, :], p["ln2_b"][None, :], alpha, batch, seq, mem_len, tiles["mem_tm"])

    dff = p["w_down"].shape[0]
    tn = tiles["ffn_tn"]
    ffp = -(-dff // tn) * tn
    w_up = p["w_up"]
    cw = p["ffn_conv_w"]
    cb = p["ffn_conv_b"][None, :]
    x3 = _ffn_ln(x2,
                 _pad_cols(w_up[:, :dff], ffp).astype(BF16), _pad_cols(w_up[:, dff:], ffp).astype(BF16),
                 _pad_cols(cw[:, :dff], ffp), _pad_cols(cw[:, dff:], ffp),
                 _pad_cols(cb[:, :dff], ffp), _pad_cols(cb[:, dff:], ffp),
                 jnp.pad(p["w_down"], ((0, ffp - dff), (0, 0))).astype(BF16),
                 p["ln3_g"][None, :], p["ln3_b"][None, :], alpha, seq, tiles["ffn_tm"], tn)
    return x3


_TILES = dict(proj_tm=1024, proj_tn=1792, merge_tm=1024, merge_tn=1024, kv_tn=512,
              mem_tm=512, ffn_tm=1024, ffn_tn=512, swa_blocks=8, scan_pairs=8)

_PER_LAYER = ("w_in", "swa_sinks", "gdn_conv_w", "gdn_a_log", "gdn_dt_bias", "gdn_norm_w", "w_br_swa",
              "w_br_gdn", "w_mix_o", "ln1_g", "ln1_b", "w_mem_q", "w_mem_kv", "w_mem_o", "ln2_g", "ln2_b",
              "w_up", "ffn_conv_w", "ffn_conv_b", "w_down", "ln3_g", "ln3_b")


def _forward(x, mem, rel_bias, weights, tiles):
    batch, seq, d = x.shape
    mem_len = mem.shape[1]
    depth = weights["w_in"].shape[0]
    alpha = (2 * depth) ** 0.25
    bias_tab = _swa_bias_table(rel_bias)
    x2d = x.reshape(batch * seq, d)
    mem2d = mem.reshape(batch * mem_len, d)
    for l in range(depth):
        p = {name: weights[name][l] for name in _PER_LAYER}
        x2d = _layer(x2d, mem2d, batch, seq, mem_len, p, bias_tab, alpha, tiles)
    return x2d.reshape(batch, seq, d)


def kernel(x, mem, w_in, rel_bias, swa_sinks, gdn_conv_w, gdn_a_log, gdn_dt_bias, gdn_norm_w, w_br_swa, w_br_gdn, w_mix_o, ln1_g, ln1_b, w_mem_q, w_mem_kv, w_mem_o, ln2_g, ln2_b, w_up, ffn_conv_w, ffn_conv_b, w_down, ln3_g, ln3_b):
    weights = dict(w_in=w_in, swa_sinks=swa_sinks, gdn_conv_w=gdn_conv_w, gdn_a_log=gdn_a_log,
                   gdn_dt_bias=gdn_dt_bias, gdn_norm_w=gdn_norm_w, w_br_swa=w_br_swa, w_br_gdn=w_br_gdn,
                   w_mix_o=w_mix_o, ln1_g=ln1_g, ln1_b=ln1_b, w_mem_q=w_mem_q, w_mem_kv=w_mem_kv,
                   w_mem_o=w_mem_o, ln2_g=ln2_g, ln2_b=ln2_b, w_up=w_up, ffn_conv_w=ffn_conv_w,
                   ffn_conv_b=ffn_conv_b, w_down=w_down, ln3_g=ln3_g, ln3_b=ln3_b)
    return _forward(x, mem, rel_bias, weights, _TILES)
```
